```python
import math
import jax, jax.numpy as jnp
from jax import lax
import numpy as np

D_MODEL = 1024
BATCH = 2
SEQ = 16384
DEPTH = 2

N_MIXERS = 2
N_LAYERS_A = (DEPTH + 1) // 2
N_LAYERS_B = DEPTH // 2
Q_BLOCK = 128
EPS = 1e-6
MASK_VALUE = -1e30
FORCED_SCORE = 1e6

NSA_HEADS = 16
NSA_KV_GROUPS = 4
NSA_HEADS_PER_GROUP = NSA_HEADS // NSA_KV_GROUPS
NSA_HEAD_DIM = D_MODEL // NSA_HEADS
CMP_STRIDE = 16
CMP_BLOCK = 2 * CMP_STRIDE
CMP_HIDDEN = 4 * NSA_HEAD_DIM
SEL_BLOCK = 64
SEL_TOP_N = 16
WINDOW = 512
NSA_Q_COLS = NSA_HEADS * NSA_HEAD_DIM
NSA_KV_COLS = NSA_KV_GROUPS * NSA_HEAD_DIM
NSA_GATE_COLS = 3 * NSA_HEADS
NSA_SPLITS = [int(v) for v in np.cumsum([NSA_Q_COLS] + [NSA_KV_COLS] * 6)]
NSA_PROJ_COLS = NSA_Q_COLS + 6 * NSA_KV_COLS + NSA_GATE_COLS

DIFF_HEADS = 8
DIFF_HEAD_DIM = D_MODEL // (2 * DIFF_HEADS)
DIFF_QK_COLS = DIFF_HEADS * 2 * DIFF_HEAD_DIM
DIFF_V_COLS = DIFF_HEADS * 2 * DIFF_HEAD_DIM
DIFF_PROJ_COLS = 2 * DIFF_QK_COLS + DIFF_V_COLS

D_FF = 2816
CONV_WIDTH = 3

kernel_name = 'hybrid_nsa_diffattn_convffn'


def rms_norm(x, g):
    xf = x.astype(jnp.float32)
    y = xf * lax.rsqrt(jnp.mean(xf * xf, axis=-1, keepdims=True) + EPS)
    return (y * g.astype(jnp.float32)).astype(x.dtype)


def alibi_slopes(n):
    return jnp.exp2(-8.0 * jnp.arange(1, n + 1, dtype=jnp.float32) / n)


def masked_softmax(s, mask):
    s = jnp.where(mask, s, MASK_VALUE)
    m = jnp.max(s, axis=-1, keepdims=True)
    e = jnp.where(mask, jnp.exp(s - m), 0.0)
    return e / jnp.maximum(jnp.sum(e, axis=-1, keepdims=True), 1e-30)


def compress_blocks(kv, pe, w1, w2):
    B, S, G, Dh = kv.shape
    chunks = kv.reshape(B, S // CMP_STRIDE, CMP_STRIDE, G, Dh)
    blocks = jnp.concatenate([chunks[:, :-1], chunks[:, 1:]], axis=2)
    blocks = blocks + pe[:, None, :]
    n_cmp = blocks.shape[1]
    flat = blocks.transpose(0, 1, 3, 2, 4).reshape(B, n_cmp, G, CMP_BLOCK * Dh)
    return jax.nn.gelu(flat @ w1) @ w2


def nsa_mixer(h, w_in, cmp_k_pe, cmp_k_w1, cmp_k_w2, cmp_v_pe, cmp_v_w1, cmp_v_w2, w_out):
    B, S, _ = h.shape
    G, P, Dh = NSA_KV_GROUPS, NSA_HEADS_PER_GROUP, NSA_HEAD_DIM
    proj = h @ w_in
    q, kc, vc, ks, vs, kw, vw, gates = jnp.split(proj, NSA_SPLITS, axis=-1)
    q = q.reshape(B, S, G, P, Dh) * (Dh ** -0.5)
    kc, vc, ks, vs, kw, vw = [a.reshape(B, S, G, Dh) for a in (kc, vc, ks, vs, kw, vw)]
    gates = jax.nn.sigmoid(gates.reshape(B, S, G, P, 3))

    k_cmp = compress_blocks(kc, cmp_k_pe, cmp_k_w1, cmp_k_w2)
    v_cmp = compress_blocks(vc, cmp_v_pe, cmp_v_w1, cmp_v_w2)
    n_cmp = k_cmp.shape[1]
    cmp_end = jnp.arange(n_cmp) * CMP_STRIDE + CMP_BLOCK - 1

    n_sel = S // SEL_BLOCK
    k_sel = ks.reshape(B, n_sel, SEL_BLOCK, G, Dh).transpose(0, 3, 1, 2, 4)
    v_sel = vs.reshape(B, n_sel, SEL_BLOCK, G, Dh).transpose(0, 3, 1, 2, 4)
    n_top = min(SEL_TOP_N, n_sel)
    ratio = SEL_BLOCK // CMP_STRIDE
    back_pad = ratio * n_sel - n_cmp

    kw_pad = jnp.pad(kw, ((0, 0), (WINDOW, 0), (0, 0), (0, 0)))
    vw_pad = jnp.pad(vw, ((0, 0), (WINDOW, 0), (0, 0), (0, 0)))

    slopes = alibi_slopes(NSA_HEADS).reshape(G, P)[None, :, :, None, None]
    b_ix = jnp.arange(B)[:, None, None, None]
    g_ix = jnp.arange(G)[None, :, None, None]
    blk_ids = jnp.arange(n_sel)

    def block_fn(blk):
        q0 = blk * Q_BLOCK
        t = q0 + jnp.arange(Q_BLOCK)
        qb = lax.dynamic_slice_in_dim(q, q0, Q_BLOCK, axis=1)
        gb = lax.dynamic_slice_in_dim(gates, q0, Q_BLOCK, axis=1)

        dist_c = (t[:, None] - cmp_end[None, :]).astype(jnp.float32)
        s_c = jnp.einsum('btgpd,bjgd->bgptj', qb, k_cmp).astype(jnp.float32) - slopes * dist_c
        p_cmp = masked_softmax(s_c, dist_c >= 0)
        o_cmp = jnp.einsum('bgptj,bjgd->btgpd', p_cmp.astype(v_cmp.dtype), v_cmp)

        p_grp = jnp.sum(p_cmp, axis=2)
        p_pad = jnp.pad(p_grp, ((0, 0), (0, 0), (0, 0), (1, back_pad)))
        imp = p_pad[..., 0:ratio * n_sel:ratio]
        for r in range(1, ratio + 1):
            imp = imp + p_pad[..., r:r + ratio * n_sel:ratio]
        cur = t // SEL_BLOCK
        valid = blk_ids[None, :] <= cur[:, None]
        forced = (blk_ids[None, :] == 0) | (valid & (blk_ids[None, :] >= cur[:, None] - 1))
        imp = jnp.where(forced, FORCED_SCORE, jnp.where(valid, imp, -1.0))
        _, sel = lax.top_k(imp, n_top)

        k_g = k_sel[b_ix, g_ix, sel].reshape(B, G, Q_BLOCK, n_top * SEL_BLOCK, Dh)
        v_g = v_sel[b_ix, g_ix, sel].reshape(B, G, Q_BLOCK, n_top * SEL_BLOCK, Dh)
        pos = (sel[..., None] * SEL_BLOCK + jnp.arange(SEL_BLOCK)).reshape(B, G, Q_BLOCK, n_top * SEL_BLOCK)
        dist_s = (t[None, None, :, None] - pos).astype(jnp.float32)[:, :, None]
        s_s = jnp.einsum('btgpd,bgtkd->bgptk', qb, k_g).astype(jnp.float32) - slopes * dist_s
        p_s = masked_softmax(s_s, dist_s >= 0)
        o_sel = jnp.einsum('bgptk,bgtkd->btgpd', p_s.astype(v_g.dtype), v_g)

        kwb = lax.dynamic_slice_in_dim(kw_pad, q0, WINDOW + Q_BLOCK, axis=1)
        vwb = lax.dynamic_slice_in_dim(vw_pad, q0, WINDOW + Q_BLOCK, axis=1)
        spos = q0 - WINDOW + jnp.arange(WINDOW + Q_BLOCK)
        dist_w = t[:, None] - spos[None, :]
        mask_w = (dist_w >= 0) & (dist_w < WINDOW) & (spos[None, :] >= 0)
        s_w = jnp.einsum('btgpd,bkgd->bgptk', qb, kwb).astype(jnp.float32) - slopes * dist_w.astype(jnp.float32)
        p_w = masked_softmax(s_w, mask_w)
        o_win = jnp.einsum('bgptk,bkgd->btgpd', p_w.astype(vwb.dtype), vwb)

        o = gb[..., 0:1] * o_cmp + gb[..., 1:2] * o_sel + gb[..., 2:3] * o_win
        return o.reshape(B, Q_BLOCK, NSA_HEADS * Dh)

    out = lax.map(block_fn, jnp.arange(S // Q_BLOCK))
    out = out.transpose(1, 0, 2, 3).reshape(B, S, NSA_HEADS * Dh)
    return out @ w_out


def diff_mixer(h, w_in, lam_q1, lam_k1, lam_q2, lam_k2, subln_g, w_out, lambda_init):
    B, S, _ = h.shape
    H, d = DIFF_HEADS, DIFF_HEAD_DIM
    proj = h @ w_in
    q, k, v = jnp.split(proj, [DIFF_QK_COLS, 2 * DIFF_QK_COLS], axis=-1)
    q = q.reshape(B, S, H, 2, d) * (d ** -0.5)
    k = k.reshape(B, S, H, 2, d)
    v = v.reshape(B, S, H, 2 * d)
    f32 = jnp.float32
    lam = (jnp.exp(jnp.sum(lam_q1.astype(f32) * lam_k1.astype(f32)))
           - jnp.exp(jnp.sum(lam_q2.astype(f32) * lam_k2.astype(f32))) + lambda_init)
    slopes = alibi_slopes(H)[None, None, :, None, None]
    kpos = jnp.arange(S)

    def block_fn(blk):
        q0 = blk * Q_BLOCK
        t = q0 + jnp.arange(Q_BLOCK)
        qb = lax.dynamic_slice_in_dim(q, q0, Q_BLOCK, axis=1)
        dist = (t[:, None] - kpos[None, :]).astype(f32)
        s = jnp.einsum('bthcd,bshcd->bchts', qb, k).astype(f32) - slopes * dist
        p = masked_softmax(s, dist >= 0)
        a = p[:, 0] - lam * p[:, 1]
        return jnp.einsum('bhts,bshe->bthe', a.astype(v.dtype), v)

    o = lax.map(block_fn, jnp.arange(S // Q_BLOCK))
    o = o.transpose(1, 0, 2, 3, 4).reshape(B, S, H, 2 * d)
    o = rms_norm(o, subln_g) * (1.0 - lambda_init)
    return o.reshape(B, S, H * 2 * d) @ w_out


def conv_ffn(h, w_up, conv_w, conv_b, w_down):
    u = h @ w_up
    c = u.shape[-1]
    u = lax.conv_general_dilated(u, conv_w[:, None, :], window_strides=(1,),
                                 padding=[(CONV_WIDTH - 1, 0)],
                                 dimension_numbers=('NWC', 'WIO', 'NWC'),
                                 feature_group_count=c) + conv_b
    gate, val = jnp.split(u, 2, axis=-1)
    return (jax.nn.silu(gate) * val) @ w_down


def setup_inputs(seed: int = 0) -> dict:
    key = jax.random.key(seed)
    ks = jax.random.split(key, 24)
    f32 = jnp.float32

    def w(k, shape, fan_in):
        return jax.random.normal(k, shape, f32) * (fan_in ** -0.5)

    def gain(k, shape):
        return 1.0 + 0.01 * jax.random.normal(k, shape, f32)

    Dh = NSA_HEAD_DIM
    d = DIFF_HEAD_DIM
    return {
        'x': jax.random.normal(ks[0], (BATCH, SEQ, D_MODEL), f32),
        'norm_mix_g': gain(ks[1], (DEPTH, D_MODEL)),
        'norm_ffn_g': gain(ks[2], (DEPTH, D_MODEL)),
        'final_norm_g': gain(ks[3], (D_MODEL,)),
        'nsa_w_in': w(ks[4], (N_LAYERS_A, D_MODEL, NSA_PROJ_COLS), D_MODEL),
        'nsa_cmp_k_pe': 0.1 * jax.random.normal(ks[5], (N_LAYERS_A, CMP_BLOCK, Dh), f32),
        'nsa_cmp_k_w1': w(ks[6], (N_LAYERS_A, CMP_BLOCK * Dh, CMP_HIDDEN), CMP_BLOCK * Dh),
        'nsa_cmp_k_w2': w(ks[7], (N_LAYERS_A, CMP_HIDDEN, Dh), CMP_HIDDEN),
        'nsa_cmp_v_pe': 0.1 * jax.random.normal(ks[8], (N_LAYERS_A, CMP_BLOCK, Dh), f32),
        'nsa_cmp_v_w1': w(ks[9], (N_LAYERS_A, CMP_BLOCK * Dh, CMP_HIDDEN), CMP_BLOCK * Dh),
        'nsa_cmp_v_w2': w(ks[10], (N_LAYERS_A, CMP_HIDDEN, Dh), CMP_HIDDEN),
        'nsa_w_out': w(ks[11], (N_LAYERS_A, NSA_HEADS * Dh, D_MODEL), NSA_HEADS * Dh),
        'diff_w_in': w(ks[12], (N_LAYERS_B, D_MODEL, DIFF_PROJ_COLS), D_MODEL),
        'diff_lam_q1': 0.1 * jax.random.normal(ks[13], (N_LAYERS_B, d), f32),
        'diff_lam_k1': 0.1 * jax.random.normal(ks[14], (N_LAYERS_B, d), f32),
        'diff_lam_q2': 0.1 * jax.random.normal(ks[15], (N_LAYERS_B, d), f32),
        'diff_lam_k2': 0.1 * jax.random.normal(ks[16], (N_LAYERS_B, d), f32),
        'diff_subln_g': gain(ks[17], (N_LAYERS_B, 2 * d)),
        'diff_w_out': w(ks[18], (N_LAYERS_B, DIFF_V_COLS, D_MODEL), DIFF_V_COLS),
        'ffn_w_up': w(ks[19], (DEPTH, D_MODEL, 2 * D_FF), D_MODEL),
        'ffn_conv_w': w(ks[20], (DEPTH, CONV_WIDTH, 2 * D_FF), CONV_WIDTH),
        'ffn_conv_b': 0.01 * jax.random.normal(ks[21], (DEPTH, 2 * D_FF), f32),
        'ffn_w_down': w(ks[22], (DEPTH, D_FF, D_MODEL), D_FF),
    }


def reference(x, norm_mix_g, norm_ffn_g, final_norm_g,
              nsa_w_in, nsa_cmp_k_pe, nsa_cmp_k_w1, nsa_cmp_k_w2,
              nsa_cmp_v_pe, nsa_cmp_v_w1, nsa_cmp_v_w2, nsa_w_out,
              diff_w_in, diff_lam_q1, diff_lam_k1, diff_lam_q2, diff_lam_k2,
              diff_subln_g, diff_w_out,
              ffn_w_up, ffn_conv_w, ffn_conv_b, ffn_w_down):
    for i in range(DEPTH):
        h = rms_norm(x, norm_mix_g[i])
        j = i // N_MIXERS
        if i % N_MIXERS == 0:
            mix = nsa_mixer(h, nsa_w_in[j], nsa_cmp_k_pe[j], nsa_cmp_k_w1[j], nsa_cmp_k_w2[j],
                            nsa_cmp_v_pe[j], nsa_cmp_v_w1[j], nsa_cmp_v_w2[j], nsa_w_out[j])
        else:
            lambda_init = 0.8 - 0.6 * math.exp(-0.3 * i)
            mix = diff_mixer(h, diff_w_in[j], diff_lam_q1[j], diff_lam_k1[j], diff_lam_q2[j],
                             diff_lam_k2[j], diff_subln_g[j], diff_w_out[j], lambda_init)
        x = x + mix
        h = rms_norm(x, norm_ffn_g[i])
        x = x + conv_ffn(h, ffn_w_up[i], ffn_conv_w[i], ffn_conv_b[i], ffn_w_down[i])
    return rms_norm(x, final_norm_g)
```

```python
import functools
import math

import jax
import jax.numpy as jnp
from jax import lax
from jax.experimental import pallas as pl
from jax.experimental.pallas import tpu as pltpu

F32 = jnp.float32
BF16 = jnp.bfloat16

D_MODEL = 1024
EPS = 1e-6
MASK_VALUE = -1e30
FORCED_SCORE = 1e6

NSA_HEADS = 16
NSA_GROUPS = 4
NSA_HPG = 4
HEAD_DIM = 64
CMP_STRIDE = 16
CMP_BLOCK = 32
CMP_HIDDEN = 256
SEL_BLOCK = 64
SEL_TOP_N = 16
WINDOW = 512
NSA_QKV_COLS = NSA_HEADS * HEAD_DIM + 6 * NSA_GROUPS * HEAD_DIM
NSA_GATE_PAD = 128
NSA_GATE_STRIDE = 16

DIFF_HEADS = 8
DIFF_V = 128

D_FF = 2816
FF_CHUNK = 256

SEL_CHUNK = 512
BLOCKS_PER_CHUNK = SEL_CHUNK // SEL_BLOCK
MASK_BIG = -(2.0 ** 100)

VMEM_LIMIT = 56 * 1024 * 1024


def _cparams(n_axes):
    return pltpu.CompilerParams(dimension_semantics=("arbitrary",) * n_axes,
                                vmem_limit_bytes=VMEM_LIMIT)


def _rms(x, g):
    return x * lax.rsqrt(jnp.mean(x * x, axis=-1, keepdims=True) + EPS) * g


def _dot(a, b):
    return jnp.dot(a, b, preferred_element_type=F32)


def _dot_nt(a, b):
    return lax.dot_general(a, b, (((1,), (1,)), ((), ())), preferred_element_type=F32)


def _nsa_proj_kernel(x_ref, g_ref, w_ref, q_ref, kv_ref, gate_ref):
    h = _rms(x_ref[0], g_ref[...]).astype(BF16)
    for c in range(NSA_QKV_COLS // 256):
        pc = _dot(h, w_ref[:, 256 * c:256 * (c + 1)])
        for s in range(4):
            piece = pc[:, 64 * s:64 * (s + 1)]
            if c < 4:
                q_ref[0, 4 * c + s] = (piece * (HEAD_DIM ** -0.5)).astype(BF16)
            else:
                kv_ref[c - 4, 0, s] = piece.astype(BF16)
    pg = _dot(h, w_ref[:, NSA_QKV_COLS:NSA_QKV_COLS + NSA_GATE_PAD])
    for g in range(NSA_GROUPS):
        gate_ref[0, g] = jax.nn.sigmoid(pg[:, NSA_GATE_STRIDE * g:NSA_GATE_STRIDE * (g + 1)])


def _nsa_proj(x, g, w, tm=512):
    B, S, D = x.shape
    n = w.shape[1]
    return pl.pallas_call(
        _nsa_proj_kernel,
        grid=(B, S // tm),
        in_specs=[pl.BlockSpec((1, tm, D), lambda b, i: (b, i, 0)),
                  pl.BlockSpec((1, D), lambda b, i: (0, 0)),
                  pl.BlockSpec((D, n), lambda b, i: (0, 0))],
        out_specs=[pl.BlockSpec((1, NSA_HEADS, tm, HEAD_DIM), lambda b, i: (b, 0, i, 0)),
                   pl.BlockSpec((6, 1, NSA_GROUPS, tm, HEAD_DIM), lambda b, i: (0, b, 0, i, 0)),
                   pl.BlockSpec((1, NSA_GROUPS, tm, NSA_GATE_STRIDE), lambda b, i: (b, 0, i, 0))],
        out_shape=[jax.ShapeDtypeStruct((B, NSA_HEADS, S, HEAD_DIM), BF16),
                   jax.ShapeDtypeStruct((6, B, NSA_GROUPS, S, HEAD_DIM), BF16),
                   jax.ShapeDtypeStruct((B, NSA_GROUPS, S, NSA_GATE_STRIDE), F32)],
        compiler_params=_cparams(2),
        name="nsa_proj",
    )(x, g, w)


def _compress_kernel(x_ref, pe_ref, w1_ref, w2_ref, o_ref, shift_ref):
    nc = x_ref.shape[3]
    half = CMP_STRIDE * HEAD_DIM
    x = x_ref[0, 0, 0]
    first = _dot(x, w1_ref[0, :half, :])
    second = _dot(x, w1_ref[0, half:, :])
    shift_ref[0:nc, :] = second
    shift_ref[nc:nc + 8, :] = jnp.zeros((8, CMP_HIDDEN), F32)
    pe_term = _dot(pe_ref[0], w1_ref[0])[0:1, :]
    pre = first + shift_ref[pl.ds(1, nc), :] + pe_term
    hid = jax.nn.gelu(pre, approximate=True)
    o_ref[0, 0, 0] = _dot(hid.astype(BF16), w2_ref[0]).astype(BF16)


def _compress(xc, pe8, w1, w2):
    _, B, G, nc, width = xc.shape
    return pl.pallas_call(
        _compress_kernel,
        grid=(2, B, G),
        in_specs=[pl.BlockSpec((1, 1, 1, nc, width), lambda a, b, g: (a, b, g, 0, 0)),
                  pl.BlockSpec((1, 8, CMP_BLOCK * HEAD_DIM), lambda a, b, g: (a, 0, 0)),
                  pl.BlockSpec((1, CMP_BLOCK * HEAD_DIM, CMP_HIDDEN), lambda a, b, g: (a, 0, 0)),
                  pl.BlockSpec((1, CMP_HIDDEN, HEAD_DIM), lambda a, b, g: (a, 0, 0))],
        out_specs=pl.BlockSpec((1, 1, 1, nc, HEAD_DIM), lambda a, b, g: (a, b, g, 0, 0)),
        out_shape=jax.ShapeDtypeStruct((2, B, G, nc, HEAD_DIM), BF16),
        scratch_shapes=[pltpu.VMEM((nc + 8, CMP_HIDDEN), F32)],
        compiler_params=_cparams(3),
        name="nsa_compress",
    )(xc, pe8, w1, w2)


def _cmp_topk_kernel(slopes_ref, q_ref, kc_ref, vct_ref, pool_ref, grp_ref,
                     ocmp_ref, nsel_ref, bits_ref, *, tq):
    g = pl.program_id(1)
    i = pl.program_id(2)
    ncp = kc_ref.shape[2]
    nsel_blocks = pool_ref.shape[0]
    t0 = i * tq
    kc = kc_ref[0, 0]
    vct = vct_ref[0, 0]

    tok = t0 + lax.broadcasted_iota(jnp.int32, (ncp, tq), 1)
    cmp_end = lax.broadcasted_iota(jnp.int32, (ncp, tq), 0) * CMP_STRIDE + (CMP_BLOCK - 1)
    dist_i = tok - cmp_end
    mask = dist_i >= 0
    dist = dist_i.astype(F32)

    p_grp = jnp.zeros((ncp, tq), F32)
    for p in range(NSA_HPG):
        slope = slopes_ref[NSA_HPG * g + p]
        s = _dot_nt(kc, q_ref[0, p]) - slope * dist
        s = jnp.where(mask, s, MASK_VALUE)
        m = jnp.max(s, axis=0, keepdims=True)
        e = jnp.where(mask, jnp.exp(s - m), 0.0)
        l = jnp.sum(e, axis=0, keepdims=True)
        prob = e * (1.0 / jnp.maximum(l, 1e-30))
        p_grp = p_grp + prob
        o_t = _dot(vct, prob.astype(BF16))
        ocmp_ref[0, p] = o_t.T

    hi = p_grp.astype(BF16)
    r1 = p_grp - hi.astype(F32)
    mid = r1.astype(BF16)
    lo = (r1 - mid.astype(F32)).astype(BF16)
    pool = pool_ref[...]
    imp = _dot(pool, hi) + _dot(pool, mid) + _dot(pool, lo)

    blk = lax.broadcasted_iota(jnp.int32, (nsel_blocks, tq), 0)
    cur = (t0 + lax.broadcasted_iota(jnp.int32, (nsel_blocks, tq), 1)) // SEL_BLOCK
    valid = blk <= cur
    forced = (blk == 0) | (valid & (blk >= cur - 1))
    work = jnp.where(forced, FORCED_SCORE, jnp.where(valid, imp, -1.0))
    blk_f = blk.astype(F32)
    sel = jnp.zeros((nsel_blocks, tq), F32)
    for _ in range(min(SEL_TOP_N, nsel_blocks)):
        mx = jnp.max(work, axis=0, keepdims=True)
        first = jnp.min(jnp.where(work == mx, blk_f, float(nsel_blocks)), axis=0, keepdims=True)
        pick = blk_f == first
        sel = jnp.where(pick, 1.0, sel)
        work = jnp.where(pick, -jnp.inf, work)
    sel = jnp.where(valid, sel, 0.0)

    sel_t = sel.T
    nsel_ref[0, 0] = (1.0 - sel_t).astype(BF16)

    any_blk = jnp.max(sel_t, axis=0, keepdims=True)
    cnt = _dot(jnp.broadcast_to(any_blk, (8, nsel_blocks)).astype(BF16), grp_ref[...])[0:1, :]
    lane = lax.broadcasted_iota(jnp.int32, (1, 128), 1)
    flag = cnt > 0.0
    w_lo = jnp.where(flag & (lane < 16), jnp.exp2(lane.astype(F32)), 0.0)
    w_hi = jnp.where(flag & (lane >= 16) & (lane < 32), jnp.exp2((lane - 16).astype(F32)), 0.0)
    lo_word = jnp.sum(w_lo, axis=1, keepdims=True)
    hi_word = jnp.sum(w_hi, axis=1, keepdims=True)
    bits_ref[0] = jnp.where(lane == 0, lo_word, jnp.where(lane == 1, hi_word, 0.0)).astype(jnp.int32)


def _cmp_topk(slopes, q, kcmp, vcmp_t, pool_t, grp, tq=128):
    B, H, S, _ = q.shape
    G = NSA_GROUPS
    ncp = kcmp.shape[2]
    nsb = pool_t.shape[0]
    nq = S // tq
    kern = functools.partial(_cmp_topk_kernel, tq=tq)
    return pl.pallas_call(
        kern,
        grid_spec=pltpu.PrefetchScalarGridSpec(
            num_scalar_prefetch=1,
            grid=(B, G, nq),
            in_specs=[pl.BlockSpec((1, NSA_HPG, tq, HEAD_DIM), lambda b, g, i, sl: (b, g, i, 0)),
                      pl.BlockSpec((1, 1, ncp, HEAD_DIM), lambda b, g, i, sl: (b, g, 0, 0)),
                      pl.BlockSpec((1, 1, HEAD_DIM, ncp), lambda b, g, i, sl: (b, g, 0, 0)),
                      pl.BlockSpec((nsb, ncp), lambda b, g, i, sl: (0, 0)),
                      pl.BlockSpec((nsb, 128), lambda b, g, i, sl: (0, 0))],
            out_specs=[pl.BlockSpec((1, NSA_HPG, tq, HEAD_DIM), lambda b, g, i, sl: (b, g, i, 0)),
                       pl.BlockSpec((1, 1, tq, nsb), lambda b, g, i, sl: (b, g, i, 0)),
                       pl.BlockSpec((1, 1, 128), lambda b, g, i, sl: ((b * G + g) * nq + i, 0, 0))]),
        out_shape=[jax.ShapeDtypeStruct((B, H, S, HEAD_DIM), F32),
                   jax.ShapeDtypeStruct((B, G, S, nsb), BF16),
                   jax.ShapeDtypeStruct((B * G * nq, 1, 128), jnp.int32)],
        compiler_params=_cparams(3),
        name="nsa_cmp_topk",
    )(slopes, q, kcmp, vcmp_t, pool_t, grp)


def _sel_win_kernel(bits_ref, slopes_ref, q_ref, ks_ref, vs_ref, kw_ref, vw_ref, nsel_ref,
                    expand_ref, ocmp_ref, gate_ref, o_ref, m_ref, l_ref, acc_ref, *, tq):
    b = pl.program_id(0)
    g = pl.program_id(1)
    i = pl.program_id(2)
    nq = pl.num_programs(2)
    S = ks_ref.shape[2]
    nsb = nsel_ref.shape[3]
    rows = NSA_HPG * tq
    t0 = i * tq
    q4 = q_ref[0].reshape(rows, HEAD_DIM)
    word = bits_ref[((b * NSA_GROUPS + g) * nq + i)]
    slope_col = jnp.concatenate(
        [jnp.full((tq, 1), slopes_ref[NSA_HPG * g + p], F32) for p in range(NSA_HPG)], axis=0)

    m_ref[...] = jnp.full((rows, 1), MASK_VALUE, F32)
    l_ref[...] = jnp.zeros((rows, 1), F32)
    acc_ref[...] = jnp.zeros((rows, HEAD_DIM), F32)
    nsel = nsel_ref[0, 0].astype(F32)
    rel = (lax.broadcasted_iota(jnp.int32, (tq, SEL_CHUNK), 1)
           - lax.broadcasted_iota(jnp.int32, (tq, SEL_CHUNK), 0))
    rel_f = rel.astype(F32)
    diag_chunk = t0 // SEL_CHUNK

    def chunk_step(c, causal):
        k0 = pl.multiple_of(c * SEL_CHUNK, SEL_CHUNK)
        kc = ks_ref[0, 0, pl.ds(k0, SEL_CHUNK), :]
        vc = vs_ref[0, 0, pl.ds(k0, SEL_CHUNK), :]
        s = _dot_nt(q4, kc)
        shift = (nsb - BLOCKS_PER_CHUNK * c) % nsb
        ns_c = pltpu.roll(nsel, shift, 1)[:, :128].astype(BF16)
        mbias = _dot(ns_c, expand_ref[...])
        if causal:
            off = t0 - k0
            mbias = jnp.where(rel > off, MASK_BIG, mbias)
        base = (t0 - k0).astype(F32)
        s = s.reshape(NSA_HPG, tq, SEL_CHUNK) + mbias[None]
        s = s.reshape(rows, SEL_CHUNK) + slope_col * jnp.tile(rel_f, (NSA_HPG, 1))
        shift_col = slope_col * base
        m_old = m_ref[...]
        m_new = jnp.maximum(m_old, jnp.max(s, axis=1, keepdims=True) - shift_col)
        p = jnp.exp(s - (m_new + shift_col))
        alpha = jnp.exp(m_old - m_new)
        l_ref[...] = alpha * l_ref[...] + jnp.sum(p, axis=1, keepdims=True)
        acc_ref[...] = alpha * acc_ref[...] + _dot(p.astype(BF16), vc)
        m_ref[...] = m_new

    def body(c, carry):
        @pl.when(((word >> c) & 1) == 1)
        def _():
            chunk_step(c, False)
        return carry

    lax.fori_loop(0, diag_chunk, body, 0)
    chunk_step(diag_chunk, True)
    o_sel = acc_ref[...] * (1.0 / jnp.maximum(l_ref[...], 1e-30))

    wlen = WINDOW + tq
    ws = pl.multiple_of(jnp.maximum(t0 - WINDOW, 0), tq)
    kwin = kw_ref[0, 0, pl.ds(ws, wlen), :]
    vwin = vw_ref[0, 0, pl.ds(ws, wlen), :]
    dist = ((t0 - ws) + lax.broadcasted_iota(jnp.int32, (tq, wlen), 0)
            - lax.broadcasted_iota(jnp.int32, (tq, wlen), 1))
    wmask = jnp.tile((dist >= 0) & (dist < WINDOW), (NSA_HPG, 1))
    sw = _dot_nt(q4, kwin) - slope_col * jnp.tile(dist.astype(F32), (NSA_HPG, 1))
    sw = jnp.where(wmask, sw, MASK_VALUE)
    mw = jnp.max(sw, axis=1, keepdims=True)
    ew = jnp.where(wmask, jnp.exp(sw - mw), 0.0)
    lw = jnp.sum(ew, axis=1, keepdims=True)
    o_win = _dot(ew.astype(BF16), vwin) * (1.0 / jnp.maximum(lw, 1e-30))

    gt = gate_ref[0, 0]
    outs = []
    for p in range(NSA_HPG):
        r = slice(p * tq, (p + 1) * tq)
        o = (gt[:, p:p + 1] * ocmp_ref[0, p]
             + gt[:, 4 + p:5 + p] * o_sel[r]
             + gt[:, 8 + p:9 + p] * o_win[r])
        outs.append(o)
    o_ref[0] = jnp.concatenate(outs, axis=1).astype(BF16)


def _sel_win(bits, slopes, q, kv, nsel, expand, ocmp, gates, tq=128):
    B, H, S, _ = q.shape
    G = NSA_GROUPS
    nsb = nsel.shape[3]
    nq = S // tq
    rows = NSA_HPG * tq
    kern = functools.partial(_sel_win_kernel, tq=tq)

    def kv_spec(branch):
        return pl.BlockSpec((1, 1, 1, S, HEAD_DIM), lambda b, g, i, *_: (branch, b, g, 0, 0))

    def kv_kernel(bits_ref, slopes_ref, q_ref, ks_ref, vs_ref, kw_ref, vw_ref, *rest):
        kern(bits_ref, slopes_ref, q_ref, ks_ref.at[0], vs_ref.at[0], kw_ref.at[0], vw_ref.at[0], *rest)

    return pl.pallas_call(
        kv_kernel,
        grid_spec=pltpu.PrefetchScalarGridSpec(
            num_scalar_prefetch=2,
            grid=(B, G, nq),
            in_specs=[pl.BlockSpec((1, NSA_HPG, tq, HEAD_DIM), lambda b, g, i, *_: (b, g, i, 0)),
                      kv_spec(2), kv_spec(3), kv_spec(4), kv_spec(5),
                      pl.BlockSpec((1, 1, tq, nsb), lambda b, g, i, *_: (b, g, i, 0)),
                      pl.BlockSpec((128, SEL_CHUNK), lambda b, g, i, *_: (0, 0)),
                      pl.BlockSpec((1, NSA_HPG, tq, HEAD_DIM), lambda b, g, i, *_: (b, g, i, 0)),
                      pl.BlockSpec((1, 1, tq, NSA_GATE_STRIDE), lambda b, g, i, *_: (b, g, i, 0))],
            out_specs=pl.BlockSpec((1, tq, NSA_HPG * HEAD_DIM), lambda b, g, i, *_: (b, i, g)),
            scratch_shapes=[pltpu.VMEM((rows, 1), F32), pltpu.VMEM((rows, 1), F32),
                            pltpu.VMEM((rows, HEAD_DIM), F32)]),
        out_shape=jax.ShapeDtypeStruct((B, S, H * HEAD_DIM), BF16),
        compiler_params=_cparams(3),
        name="nsa_sel_win",
    )(bits, slopes, q, kv, kv, kv, kv, nsel, expand, ocmp, gates)


def _out_proj_kernel(o_ref, w_ref, x_ref, y_ref):
    y_ref[0] = x_ref[0] + _dot(o_ref[0], w_ref[...])


def _out_proj(o, w, x, tm=512):
    B, S, D = x.shape
    return pl.pallas_call(
        _out_proj_kernel,
        grid=(B, S // tm),
        in_specs=[pl.BlockSpec((1, tm, o.shape[2]), lambda b, i: (b, i, 0)),
                  pl.BlockSpec(w.shape, lambda b, i: (0, 0)),
                  pl.BlockSpec((1, tm, D), lambda b, i: (b, i, 0))],
        out_specs=pl.BlockSpec((1, tm, D), lambda b, i: (b, i, 0)),
        out_shape=jax.ShapeDtypeStruct((B, S, D), F32),
        compiler_params=_cparams(2),
        name="out_proj",
    )(o, w, x)


def _norm_proj_kernel(x_ref, g_ref, w_ref, y_ref):
    h = _rms(x_ref[0], g_ref[...]).astype(BF16)
    n = w_ref.shape[1]
    for c in range(n // 512):
        y_ref[0, :, 512 * c:512 * (c + 1)] = _dot(h, w_ref[:, 512 * c:512 * (c + 1)]).astype(BF16)


def _norm_proj(x, g, w, tm=512):
    B, S, D = x.shape
    n = w.shape[1]
    return pl.pallas_call(
        _norm_proj_kernel,
        grid=(B, S // tm),
        in_specs=[pl.BlockSpec((1, tm, D), lambda b, i: (b, i, 0)),
                  pl.BlockSpec((1, D), lambda b, i: (0, 0)),
                  pl.BlockSpec((D, n), lambda b, i: (0, 0))],
        out_specs=pl.BlockSpec((1, tm, n), lambda b, i: (b, i, 0)),
        out_shape=jax.ShapeDtypeStruct((B, S, n), BF16),
        compiler_params=_cparams(2),
        name="diff_proj",
    )(x, g, w)


def _diff_attn_kernel(slopes_ref, q_ref, k_ref, v_ref, lam_ref, sg_ref, o_ref, *, tq, lambda_init):
    h = pl.program_id(1)
    i = pl.program_id(2)
    slope = slopes_ref[h]
    q = q_ref[0]
    lane = lax.broadcasted_iota(jnp.int32, q.shape, 1)
    zero = jnp.zeros_like(q)
    qs = (jnp.where(lane < HEAD_DIM, q, zero), jnp.where(lane >= HEAD_DIM, q, zero))
    rel = (lax.broadcasted_iota(jnp.int32, (tq, tq), 1)
           - lax.broadcasted_iota(jnp.int32, (tq, tq), 0))
    alibi = slope * rel.astype(F32)

    def step(j, carry, diag):
        k0 = pl.multiple_of(j * tq, tq)
        kt = k_ref[0, pl.ds(k0, tq), :]
        vt = v_ref[0, pl.ds(k0, tq), :]
        base = slope * ((i - j) * tq).astype(F32)
        new = []
        for c in range(2):
            m_old, l_old, acc = carry[c]
            s = _dot_nt(qs[c], kt) + alibi
            if diag:
                s = jnp.where(rel <= 0, s, MASK_VALUE)
            m_new = jnp.maximum(m_old, jnp.max(s, axis=1, keepdims=True) - base)
            p = jnp.exp(s - (m_new + base))
            alpha = jnp.exp(m_old - m_new)
            l_new = alpha * l_old + jnp.sum(p, axis=1, keepdims=True)
            acc = alpha * acc + _dot(p.astype(BF16), vt)
            new.append((m_new, l_new, acc))
        return tuple(new)

    init = tuple((jnp.full((tq, 1), MASK_VALUE, F32), jnp.zeros((tq, 1), F32),
                  jnp.zeros((tq, DIFF_V), F32)) for _ in range(2))
    carry = lax.fori_loop(0, i, lambda j, c: step(j, c, False), init)
    carry = step(i, carry, True)

    lv = lam_ref[...]
    lam = (jnp.exp(jnp.sum(lv[0:1] * lv[1:2], axis=1, keepdims=True))
           - jnp.exp(jnp.sum(lv[2:3] * lv[3:4], axis=1, keepdims=True)) + lambda_init)
    (_, l0, a0), (_, l1, a1) = carry
    o = a0 * (1.0 / jnp.maximum(l0, 1e-30)) - lam * (a1 * (1.0 / jnp.maximum(l1, 1e-30)))
    o = _rms(o, sg_ref[...]) * (1.0 - lambda_init)
    o_ref[0] = o.astype(BF16)


def _diff_attn(slopes, proj, lam_vecs, subln_g, lambda_init, tq=512):
    B, S, _ = proj.shape
    H = DIFF_HEADS
    kern = functools.partial(_diff_attn_kernel, tq=tq, lambda_init=lambda_init)
    return pl.pallas_call(
        kern,
        grid_spec=pltpu.PrefetchScalarGridSpec(
            num_scalar_prefetch=1,
            grid=(B, H, S // tq),
            in_specs=[pl.BlockSpec((1, tq, DIFF_V), lambda b, h, i, sl: (b, i, h)),
                      pl.BlockSpec((1, S, DIFF_V), lambda b, h, i, sl: (b, 0, H + h)),
                      pl.BlockSpec((1, S, DIFF_V), lambda b, h, i, sl: (b, 0, 2 * H + h)),
                      pl.BlockSpec((4, HEAD_DIM), lambda b, h, i, sl: (0, 0)),
                      pl.BlockSpec((1, DIFF_V), lambda b, h, i, sl: (0, 0))],
            out_specs=pl.BlockSpec((1, tq, DIFF_V), lambda b, h, i, sl: (b, i, h))),
        out_shape=jax.ShapeDtypeStruct((B, S, H * DIFF_V), BF16),
        compiler_params=_cparams(3),
        name="diff_attn",
    )(slopes, proj, proj, proj, lam_vecs, subln_g)


def _ffn_kernel(x_ref, g_ref, wup_ref, cw_ref, cb_ref, wdn_ref, fg_ref, y_ref, ug_ref, uv_ref,
                *, tm, final_norm):
    i = pl.program_id(1)

    @pl.when(i == 0)
    def _():
        ug_ref[0:8, :] = jnp.zeros((8, D_FF), F32)
        uv_ref[0:8, :] = jnp.zeros((8, D_FF), F32)

    x = x_ref[0]
    h = _rms(x, g_ref[...]).astype(BF16)
    acc = jnp.zeros((tm, D_MODEL), F32)
    for c in range(D_FF // FF_CHUNK):
        halves = []
        for u_ref, col0 in ((ug_ref, FF_CHUNK * c), (uv_ref, D_FF + FF_CHUNK * c)):
            cols = slice(FF_CHUNK * c, FF_CHUNK * (c + 1))
            u_ref[8:8 + tm, cols] = _dot(h, wup_ref[:, col0:col0 + FF_CHUNK])
            w = cw_ref[:, col0:col0 + FF_CHUNK]
            y = (u_ref[6:6 + tm, cols] * w[0:1] + u_ref[7:7 + tm, cols] * w[1:2]
                 + u_ref[8:8 + tm, cols] * w[2:3] + cb_ref[:, col0:col0 + FF_CHUNK])
            u_ref[0:8, cols] = u_ref[tm:tm + 8, cols]
            halves.append(y)
        act = (jax.nn.silu(halves[0]) * halves[1]).astype(BF16)
        acc = acc + _dot(act, wdn_ref[FF_CHUNK * c:FF_CHUNK * (c + 1), :])
    out = x + acc
    if final_norm:
        out = _rms(out, fg_ref[...])
    y_ref[0] = out


def _ffn(x, g, w_up, conv_w, conv_b, w_down, final_g, final_norm, tm=256):
    B, S, D = x.shape
    kern = functools.partial(_ffn_kernel, tm=tm, final_norm=final_norm)
    const = lambda b, i: (0, 0)
    return pl.pallas_call(
        kern,
        grid=(B, S // tm),
        in_specs=[pl.BlockSpec((1, tm, D), lambda b, i: (b, i, 0)),
                  pl.BlockSpec((1, D), const),
                  pl.BlockSpec(w_up.shape, const, pipeline_mode=pl.Buffered(1)),
                  pl.BlockSpec(conv_w.shape, const),
                  pl.BlockSpec(conv_b.shape, const),
                  pl.BlockSpec(w_down.shape, const, pipeline_mode=pl.Buffered(1)),
                  pl.BlockSpec((1, D), const)],
        out_specs=pl.BlockSpec((1, tm, D), lambda b, i: (b, i, 0)),
        out_shape=jax.ShapeDtypeStruct((B, S, D), F32),
        scratch_shapes=[pltpu.VMEM((tm + 8, D_FF), F32), pltpu.VMEM((tm + 8, D_FF), F32)],
        compiler_params=_cparams(2),
        name="conv_ffn",
    )(x, g, w_up, conv_w, conv_b, w_down, final_g)


def _alibi_slopes(n):
    return jnp.exp2(-8.0 * jnp.arange(1, n + 1, dtype=F32) / n)


def _nsa_weight(w_in):
    gates = w_in[:, NSA_QKV_COLS:].reshape(D_MODEL, NSA_GROUPS, NSA_HPG, 3)
    gates = gates.transpose(0, 1, 3, 2).reshape(D_MODEL, NSA_GROUPS, 3 * NSA_HPG)
    gates = jnp.pad(gates, ((0, 0), (0, 0), (0, NSA_GATE_STRIDE - 3 * NSA_HPG)))
    gates = gates.reshape(D_MODEL, NSA_GROUPS * NSA_GATE_STRIDE)
    gates = jnp.pad(gates, ((0, 0), (0, NSA_GATE_PAD - NSA_GROUPS * NSA_GATE_STRIDE)))
    return jnp.concatenate([w_in[:, :NSA_QKV_COLS], gates], axis=1).astype(BF16)


def _nsa_layer(x, norm_g, w_in, k_pe, k_w1, k_w2, v_pe, v_w1, v_w2, w_out):
    B, S, _ = x.shape
    q, kv, gates = _nsa_proj(x, norm_g[None], _nsa_weight(w_in))

    nc = S // CMP_STRIDE
    xc = kv[0:2].reshape(2, B, NSA_GROUPS, nc, CMP_STRIDE * HEAD_DIM)
    pe8 = jnp.broadcast_to(jnp.stack([k_pe, v_pe]).reshape(2, 1, CMP_BLOCK * HEAD_DIM),
                           (2, 8, CMP_BLOCK * HEAD_DIM)).astype(BF16)
    cmp = _compress(xc, pe8, jnp.stack([k_w1, v_w1]).astype(BF16), jnp.stack([k_w2, v_w2]).astype(BF16))
    kcmp, vcmp_t = cmp[0], cmp[1].transpose(0, 1, 3, 2)

    nsb = S // SEL_BLOCK
    ratio = SEL_BLOCK // CMP_STRIDE
    sb = jnp.arange(nsb)[:, None]
    cb = jnp.arange(nc)[None, :]
    pool_t = ((cb >= ratio * sb - 1) & (cb <= ratio * sb + ratio - 1)).astype(BF16)
    grp = (jnp.arange(nsb)[:, None] // BLOCKS_PER_CHUNK == jnp.arange(128)[None, :]).astype(BF16)
    slopes = _alibi_slopes(NSA_HEADS)
    ocmp, nsel, bits = _cmp_topk(slopes, q, kcmp, vcmp_t, pool_t, grp)
    word = bits[:, 0, 0] | (bits[:, 0, 1] << 16)

    expand = jnp.where((jnp.arange(128)[:, None] == jnp.arange(SEL_CHUNK)[None, :] // SEL_BLOCK),
                       MASK_BIG, 0.0).astype(BF16)
    o = _sel_win(word, slopes, q, kv, nsel, expand, ocmp, gates)
    return _out_proj(o, w_out.astype(BF16), x)


def _diff_layer(x, norm_g, w_in, lq1, lk1, lq2, lk2, subln_g, w_out, lambda_init):
    scale = jnp.concatenate([jnp.full((DIFF_HEADS * DIFF_V,), HEAD_DIM ** -0.5, F32),
                             jnp.ones((2 * DIFF_HEADS * DIFF_V,), F32)])
    proj = _norm_proj(x, norm_g[None], (w_in * scale[None, :]).astype(BF16))
    lam_vecs = jnp.stack([lq1, lk1, lq2, lk2]).astype(F32)
    o = _diff_attn(_alibi_slopes(DIFF_HEADS), proj, lam_vecs, subln_g[None].astype(F32), lambda_init)
    return _out_proj(o, w_out.astype(BF16), x)


def kernel(x, norm_mix_g, norm_ffn_g, final_norm_g, nsa_w_in, nsa_cmp_k_pe, nsa_cmp_k_w1, nsa_cmp_k_w2, nsa_cmp_v_pe, nsa_cmp_v_w1, nsa_cmp_v_w2, nsa_w_out, diff_w_in, diff_lam_q1, diff_lam_k1, diff_lam_q2, diff_lam_k2, diff_subln_g, diff_w_out, ffn_w_up, ffn_conv_w, ffn_conv_b, ffn_w_down):
    depth = norm_mix_g.shape[0]
    for i in range(depth):
        j = i // 2
        if i % 2 == 0:
            x = _nsa_layer(x, norm_mix_g[i], nsa_w_in[j], nsa_cmp_k_pe[j], nsa_cmp_k_w1[j], nsa_cmp_k_w2[j],
                           nsa_cmp_v_pe[j], nsa_cmp_v_w1[j], nsa_cmp_v_w2[j], nsa_w_out[j])
        else:
            lambda_init = 0.8 - 0.6 * math.exp(-0.3 * i)
            x = _diff_layer(x, norm_mix_g[i], diff_w_in[j], diff_lam_q1[j], diff_lam_k1[j], diff_lam_q2[j],
                            diff_lam_k2[j], diff_subln_g[j], diff_w_out[j], lambda_init)
        x = _ffn(x, norm_ffn_g[i][None], ffn_w_up[i].astype(BF16), ffn_conv_w[i], ffn_conv_b[i][None],
                 ffn_w_down[i].astype(BF16), final_norm_g[None], final_norm=(i == depth - 1))
    return x
```

```python
import functools
import math

import jax
import jax.numpy as jnp
from jax import lax
from jax.experimental import pallas as pl
from jax.experimental.pallas import tpu as pltpu

F32 = jnp.float32
BF16 = jnp.bfloat16

D_MODEL = 1024
EPS = 1e-6
MASK_VALUE = -1e30
FORCED_SCORE = 1e6

NSA_HEADS = 16
NSA_GROUPS = 4
NSA_HPG = 4
HEAD_DIM = 64
CMP_STRIDE = 16
CMP_BLOCK = 32
CMP_HIDDEN = 256
SEL_BLOCK = 64
SEL_TOP_N = 16
WINDOW = 512
NSA_QKV_COLS = NSA_HEADS * HEAD_DIM + 6 * NSA_GROUPS * HEAD_DIM
NSA_GATE_PAD = 128
NSA_GATE_STRIDE = 16

DIFF_HEADS = 8
DIFF_V = 128
DIFF_QK = 256
POS_PERIOD = 512
ROW_BLOCK = 32
DIFF_GROUP = 3
LOG2E = 1.4426950408889634

D_FF = 2816
FF_CHUNK = 256

SEL_CHUNK = 512
BLOCKS_PER_CHUNK = SEL_CHUNK // SEL_BLOCK
MASK_BIG = -(2.0 ** 100)
NO_KEY_FLOOR = -1e29
NSA_ROW = 128
COL_COEF = 64
COL_MASK = 70
CMP_CHUNK = 256

VMEM_LIMIT = 56 * 1024 * 1024


def _cparams(n_axes):
    return pltpu.CompilerParams(dimension_semantics=("arbitrary",) * n_axes,
                                vmem_limit_bytes=VMEM_LIMIT)


def _rms(x, g):
    return x * lax.rsqrt(jnp.mean(x * x, axis=-1, keepdims=True) + EPS) * g


def _dot(a, b):
    return jnp.dot(a, b, preferred_element_type=F32)


def _dot_nt(a, b):
    return lax.dot_general(a, b, (((1,), (1,)), ((), ())), preferred_element_type=F32)


def _nsa_proj_kernel(x_ref, g_ref, w_ref, qc_ref, kc_ref, q_ref, k_ref, kv_ref, gate_ref):
    h = _rms(x_ref[0], g_ref[...]).astype(BF16)
    col = 0
    for pair in range(NSA_HEADS // 2):
        pc = _dot(h, w_ref[:, col:col + 2 * NSA_ROW]) * (HEAD_DIM ** -0.5 * LOG2E)
        for s in range(2):
            hd = 2 * pair + s
            q_ref[0, hd] = (pc[:, NSA_ROW * s:NSA_ROW * (s + 1)] + qc_ref[hd:hd + 1, :]).astype(BF16)
        col += 2 * NSA_ROW
    for br in range(2):
        for pair in range(NSA_GROUPS // 2):
            pc = _dot(h, w_ref[:, col:col + 2 * NSA_ROW])
            for s in range(2):
                piece = pc[:, NSA_ROW * s:NSA_ROW * (s + 1)]
                k_ref[br, 0, 2 * pair + s] = (piece + kc_ref[...] if br == 0 else piece).astype(BF16)
            col += 2 * NSA_ROW
    for br in range(4):
        pc = _dot(h, w_ref[:, col:col + 256])
        for g in range(NSA_GROUPS):
            kv_ref[br, 0, g] = pc[:, 64 * g:64 * (g + 1)].astype(BF16)
        col += 256
    pg = _dot(h, w_ref[:, col:col + NSA_GATE_PAD])
    for g in range(NSA_GROUPS):
        gate_ref[0, g] = jax.nn.sigmoid(pg[:, NSA_GATE_STRIDE * g:NSA_GATE_STRIDE * (g + 1)])


def _nsa_proj(x, g, w, q_cols, k_cols):
    B, S, D = x.shape
    tm = SEL_CHUNK
    n = w.shape[1]
    return pl.pallas_call(
        _nsa_proj_kernel,
        grid=(B, S // tm),
        in_specs=[pl.BlockSpec((1, tm, D), lambda b, i: (b, i, 0)),
                  pl.BlockSpec((1, D), lambda b, i: (0, 0)),
                  pl.BlockSpec((D, n), lambda b, i: (0, 0)),
                  pl.BlockSpec(q_cols.shape, lambda b, i: (0, 0)),
                  pl.BlockSpec(k_cols.shape, lambda b, i: (0, 0))],
        out_specs=[pl.BlockSpec((1, NSA_HEADS, tm, NSA_ROW), lambda b, i: (b, 0, i, 0)),
                   pl.BlockSpec((2, 1, NSA_GROUPS, tm, NSA_ROW), lambda b, i: (0, b, 0, i, 0)),
                   pl.BlockSpec((4, 1, NSA_GROUPS, tm, HEAD_DIM), lambda b, i: (0, b, 0, i, 0)),
                   pl.BlockSpec((1, NSA_GROUPS, tm, NSA_GATE_STRIDE), lambda b, i: (b, 0, i, 0))],
        out_shape=[jax.ShapeDtypeStruct((B, NSA_HEADS, S, NSA_ROW), BF16),
                   jax.ShapeDtypeStruct((2, B, NSA_GROUPS, S, NSA_ROW), BF16),
                   jax.ShapeDtypeStruct((4, B, NSA_GROUPS, S, HEAD_DIM), BF16),
                   jax.ShapeDtypeStruct((B, NSA_GROUPS, S, NSA_GATE_STRIDE), F32)],
        compiler_params=_cparams(2),
        name="nsa_proj",
    )(x, g, w, q_cols, k_cols)


def _compress_kernel(x_ref, pe_ref, w1_ref, w2_ref, cols_ref, o_ref, shift_ref):
    nc = x_ref.shape[3]
    half = CMP_STRIDE * HEAD_DIM
    x = x_ref[0, 0, 0]
    first = _dot(x, w1_ref[0, :half, :])
    second = _dot(x, w1_ref[0, half:, :])
    shift_ref[0:nc, :] = second
    shift_ref[nc:nc + 8, :] = jnp.zeros((8, CMP_HIDDEN), F32)
    pe_term = _dot(pe_ref[0], w1_ref[0])[0:1, :]
    pre = first + shift_ref[pl.ds(1, nc), :] + pe_term
    hid = jax.nn.gelu(pre, approximate=True)
    o_ref[0, 0, 0] = (_dot(hid.astype(BF16), w2_ref[0]) + cols_ref[0]).astype(BF16)


def _compress(xc, pe8, w1, w2, cols):
    _, B, G, nc, width = xc.shape
    return pl.pallas_call(
        _compress_kernel,
        grid=(2, B, G),
        in_specs=[pl.BlockSpec((1, 1, 1, nc, width), lambda a, b, g: (a, b, g, 0, 0)),
                  pl.BlockSpec((1, 8, CMP_BLOCK * HEAD_DIM), lambda a, b, g: (a, 0, 0)),
                  pl.BlockSpec((1, CMP_BLOCK * HEAD_DIM, CMP_HIDDEN), lambda a, b, g: (a, 0, 0)),
                  pl.BlockSpec((1, CMP_HIDDEN, NSA_ROW), lambda a, b, g: (a, 0, 0)),
                  pl.BlockSpec((1, nc, NSA_ROW), lambda a, b, g: (a, 0, 0))],
        out_specs=pl.BlockSpec((1, 1, 1, nc, NSA_ROW), lambda a, b, g: (a, b, g, 0, 0)),
        out_shape=jax.ShapeDtypeStruct((2, B, G, nc, NSA_ROW), BF16),
        scratch_shapes=[pltpu.VMEM((nc + 8, CMP_HIDDEN), F32)],
        compiler_params=_cparams(3),
        name="nsa_compress",
    )(xc, pe8, w1, w2, cols)


def _cmp_topk_kernel(q_ref, kc_ref, vct_ref, pool_ref, grp_ref, ocmp_ref, nsel_ref, bits_ref, *, tq):
    i = pl.program_id(2)
    ncp = kc_ref.shape[2]
    nsel_blocks = pool_ref.shape[0]
    t0 = i * tq
    ratio = SEL_BLOCK // CMP_STRIDE

    def visible_prefix(k):
        rows = CMP_CHUNK * k
        nsb = rows // ratio
        kc = kc_ref[0, 0, 0:rows, :]
        vct = vct_ref[0, 0, :, 0:rows]
        tok = t0 + lax.broadcasted_iota(jnp.int32, (rows, tq), 1)
        cmp_end = lax.broadcasted_iota(jnp.int32, (rows, tq), 0) * CMP_STRIDE + (CMP_BLOCK - 1)
        mask = tok >= cmp_end

        p_grp = jnp.zeros((rows, tq), F32)
        for p in range(NSA_HPG):
            s = jnp.where(mask, _dot_nt(kc, q_ref[0, p]), MASK_VALUE)
            m = jnp.maximum(jnp.max(s, axis=0, keepdims=True), NO_KEY_FLOOR)
            e = jnp.exp2(s - m)
            r = 1.0 / jnp.maximum(jnp.sum(e, axis=0, keepdims=True), 1e-30)
            p_grp = p_grp + e * r
            ocmp_ref[0, p] = (_dot(vct, e.astype(BF16)) * r).T

        hi = p_grp.astype(BF16)
        r1 = p_grp - hi.astype(F32)
        mid = r1.astype(BF16)
        lo = (r1 - mid.astype(F32)).astype(BF16)
        pool = pool_ref[0:nsb, 0:rows]
        imp = _dot(pool, hi) + _dot(pool, mid) + _dot(pool, lo)

        blk = lax.broadcasted_iota(jnp.int32, (nsb, tq), 0)
        cur = (t0 + lax.broadcasted_iota(jnp.int32, (nsb, tq), 1)) // SEL_BLOCK
        valid = blk <= cur
        forced = (blk == 0) | (valid & (blk >= cur - 1))
        work = jnp.where(forced, FORCED_SCORE, jnp.where(valid, imp, -1.0))
        blk_f = blk.astype(F32)
        sel = jnp.zeros((nsb, tq), F32)
        for _ in range(min(SEL_TOP_N, nsb)):
            mx = jnp.max(work, axis=0, keepdims=True)
            first = jnp.min(jnp.where(work == mx, blk_f, float(nsb)), axis=0, keepdims=True)
            pick = blk_f == first
            sel = jnp.where(pick, 1.0, sel)
            work = jnp.where(pick, -jnp.inf, work)
        sel = jnp.where(valid, sel, 0.0)
        if nsb < nsel_blocks:
            sel = jnp.concatenate([sel, jnp.zeros((nsel_blocks - nsb, tq), F32)], axis=0)

        sel_t = sel.T
        nsel_ref[0, 0] = (1.0 - sel_t).astype(BF16)

        any_blk = jnp.max(sel_t, axis=0, keepdims=True)
        cnt = _dot(jnp.broadcast_to(any_blk, (8, nsel_blocks)).astype(BF16), grp_ref[...])[0:1, :]
        lane = lax.broadcasted_iota(jnp.int32, (1, 128), 1)
        flag = cnt > 0.0
        w_lo = jnp.where(flag & (lane < 16), jnp.exp2(lane.astype(F32)), 0.0)
        w_hi = jnp.where(flag & (lane >= 16) & (lane < 32), jnp.exp2((lane - 16).astype(F32)), 0.0)
        lo_word = jnp.sum(w_lo, axis=1, keepdims=True)
        hi_word = jnp.sum(w_hi, axis=1, keepdims=True)
        bits_ref[0] = jnp.where(lane == 0, lo_word, jnp.where(lane == 1, hi_word, 0.0)).astype(jnp.int32)

    n_chunks = ncp // CMP_CHUNK
    last_visible = (t0 + tq - CMP_BLOCK) // CMP_STRIDE
    lax.switch(jnp.clip(last_visible // CMP_CHUNK, 0, n_chunks - 1),
               [functools.partial(visible_prefix, k) for k in range(1, n_chunks + 1)])


def _cmp_topk(q, kcmp, vcmp_t, pool_t, grp, tq=128):
    B, H, S, _ = q.shape
    G = NSA_GROUPS
    ncp = kcmp.shape[2]
    nsb = pool_t.shape[0]
    nq = S // tq
    kern = functools.partial(_cmp_topk_kernel, tq=tq)
    return pl.pallas_call(
        kern,
        grid=(B, G, nq),
        in_specs=[pl.BlockSpec((1, NSA_HPG, tq, NSA_ROW), lambda b, g, i: (b, g, i, 0)),
                  pl.BlockSpec((1, 1, ncp, NSA_ROW), lambda b, g, i: (b, g, 0, 0)),
                  pl.BlockSpec((1, 1, HEAD_DIM, ncp), lambda b, g, i: (b, g, 0, 0)),
                  pl.BlockSpec((nsb, ncp), lambda b, g, i: (0, 0)),
                  pl.BlockSpec((nsb, 128), lambda b, g, i: (0, 0))],
        out_specs=[pl.BlockSpec((1, NSA_HPG, tq, HEAD_DIM), lambda b, g, i: (b, g, i, 0)),
                   pl.BlockSpec((1, 1, tq, nsb), lambda b, g, i: (b, g, i, 0)),
                   pl.BlockSpec((1, 1, 128), lambda b, g, i: ((b * G + g) * nq + i, 0, 0))],
        out_shape=[jax.ShapeDtypeStruct((B, H, S, HEAD_DIM), F32),
                   jax.ShapeDtypeStruct((B, G, S, nsb), BF16),
                   jax.ShapeDtypeStruct((B * G * nq, 1, 128), jnp.int32)],
        compiler_params=_cparams(3),
        name="nsa_cmp_topk",
    )(q, kcmp, vcmp_t, pool_t, grp)


def _sel_win_kernel(bits_ref, coef_ref, q_ref, ks_ref, kw_ref, vs_ref, vw_ref, nsel_ref,
                    ocmp_ref, gate_ref, o_ref, s_ref, p_ref, alpha_ref, m_ref, l_ref, acc_ref, *, tq):
    b = pl.program_id(0)
    g = pl.program_id(1)
    i = pl.program_id(2)
    nq = pl.num_programs(2)
    nsb = nsel_ref.shape[3]
    rows = NSA_HPG * tq
    t0 = i * tq
    q4 = q_ref[0].reshape(rows, NSA_ROW)
    word = bits_ref[((b * NSA_GROUPS + g) * nq + i)]
    coefs = [coef_ref[NSA_HPG * g + p] for p in range(NSA_HPG)]
    coef_rows = jnp.concatenate([jnp.full((tq, 128), c, F32) for c in coefs], axis=0)

    m_ref[...] = jnp.full(m_ref.shape, MASK_VALUE, F32)
    l_ref[...] = jnp.zeros(l_ref.shape, F32)
    acc_ref[...] = jnp.zeros(acc_ref.shape, F32)
    nsel = nsel_ref[0, 0].astype(F32)
    lane = lax.broadcasted_iota(jnp.int32, (tq, NSA_ROW), 1)
    flag_lanes = (lane >= COL_MASK) & (lane < COL_MASK + BLOCKS_PER_CHUNK)
    diag_chunk = t0 // SEL_CHUNK

    def chunk_step(c, causal):
        k0 = pl.multiple_of(c * SEL_CHUNK, SEL_CHUNK)
        kc = ks_ref[0, 0, pl.ds(k0, SEL_CHUNK), :]
        vc = vs_ref[0, 0, pl.ds(k0, SEL_CHUNK), :]
        shift = (COL_MASK + nsb - BLOCKS_PER_CHUNK * c) % nsb
        flags = jnp.where(flag_lanes, pltpu.roll(nsel, shift, 1)[:, :NSA_ROW], 0.0).astype(BF16)
        s_ref[...] = _dot_nt(q4 + jnp.tile(flags, (NSA_HPG, 1)), kc)
        base = coef_rows * (k0 - t0).astype(F32)
        for r in range(rows // ROW_BLOCK):
            rr = slice(r * ROW_BLOCK, (r + 1) * ROW_BLOCK)
            s = s_ref[rr, :]
            if causal:
                col = lax.broadcasted_iota(jnp.int32, s.shape, 1)
                row = (r * ROW_BLOCK) % tq + lax.broadcasted_iota(jnp.int32, s.shape, 0)
                s = jnp.where(col - row <= t0 - k0, s, MASK_VALUE)
            m_old = m_ref[rr, :]
            m_new = jnp.maximum(m_old, jnp.max(s, axis=1, keepdims=True) + base[rr])
            p = jnp.exp2(s - pltpu.repeat(m_new - base[rr], SEL_CHUNK // 128, axis=1))
            alpha = jnp.exp2(m_old - m_new)
            m_ref[rr, :] = m_new
            l_ref[rr, :] = alpha * l_ref[rr, :] + sum(p[:, 128 * t:128 * (t + 1)] for t in range(SEL_CHUNK // 128))
            alpha_ref[rr, :] = alpha
            p_ref[rr, :] = p.astype(BF16)
        acc_ref[...] = alpha_ref[:, 0:HEAD_DIM] * acc_ref[...] + _dot(p_ref[...], vc)

    def body(c, carry):
        @pl.when(((word >> c) & 1) == 1)
        def _():
            chunk_step(c, False)
        return carry

    lax.fori_loop(0, diag_chunk, body, 0)
    chunk_step(diag_chunk, True)
    l_sel = jnp.sum(l_ref[...], axis=1, keepdims=True)
    o_sel = acc_ref[...] * (1.0 / jnp.maximum(l_sel, 1e-30))

    wlen = WINDOW + tq
    ws = pl.multiple_of(jnp.maximum(t0 - WINDOW, 0), tq)
    kwin = kw_ref[0, 0, pl.ds(ws, wlen), :]
    vwin = vw_ref[0, 0, pl.ds(ws, wlen), :]
    dist = ((t0 - ws) + lax.broadcasted_iota(jnp.int32, (tq, wlen), 0)
            - lax.broadcasted_iota(jnp.int32, (tq, wlen), 1))
    wmask = (dist >= 0) & (dist < WINDOW)
    dist_f = dist.astype(F32)
    wbias = jnp.concatenate([jnp.where(wmask, -c * dist_f, MASK_VALUE) for c in coefs], axis=0)
    sw = _dot_nt(q4, kwin) + wbias
    ew = jnp.exp2(sw - jnp.max(sw, axis=1, keepdims=True))
    lw = jnp.sum(ew, axis=1, keepdims=True)
    o_win = _dot(ew.astype(BF16), vwin) * (1.0 / jnp.maximum(lw, 1e-30))

    gt = gate_ref[0, 0]
    outs = []
    for p in range(NSA_HPG):
        r = slice(p * tq, (p + 1) * tq)
        o = (gt[:, p:p + 1] * ocmp_ref[0, p]
             + gt[:, 4 + p:5 + p] * o_sel[r]
             + gt[:, 8 + p:9 + p] * o_win[r])
        outs.append(o)
    o_ref[0] = jnp.concatenate(outs, axis=1).astype(BF16)


def _sel_win(bits, coefs, q, k128, kv, nsel, ocmp, gates, tq=128):
    B, H, S, _ = q.shape
    G = NSA_GROUPS
    nsb = nsel.shape[3]
    nq = S // tq
    rows = NSA_HPG * tq
    kern = functools.partial(_sel_win_kernel, tq=tq)

    def resident(branch, width):
        return pl.BlockSpec((1, 1, 1, S, width), lambda b, g, i, *_: (branch, b, g, 0, 0))

    def kv_kernel(bits_ref, coef_ref, q_ref, ks_ref, kw_ref, vs_ref, vw_ref, *rest):
        kern(bits_ref, coef_ref, q_ref, ks_ref.at[0], kw_ref.at[0], vs_ref.at[0], vw_ref.at[0], *rest)

    return pl.pallas_call(
        kv_kernel,
        grid_spec=pltpu.PrefetchScalarGridSpec(
            num_scalar_prefetch=2,
            grid=(B, G, nq),
            in_specs=[pl.BlockSpec((1, NSA_HPG, tq, NSA_ROW), lambda b, g, i, *_: (b, g, i, 0)),
                      resident(0, NSA_ROW), resident(1, NSA_ROW),
                      resident(2, HEAD_DIM), resident(3, HEAD_DIM),
                      pl.BlockSpec((1, 1, tq, nsb), lambda b, g, i, *_: (b, g, i, 0)),
                      pl.BlockSpec((1, NSA_HPG, tq, HEAD_DIM), lambda b, g, i, *_: (b, g, i, 0)),
                      pl.BlockSpec((1, 1, tq, NSA_GATE_STRIDE), lambda b, g, i, *_: (b, g, i, 0))],
            out_specs=pl.BlockSpec((1, tq, NSA_HPG * HEAD_DIM), lambda b, g, i, *_: (b, i, g)),
            scratch_shapes=[pltpu.VMEM((rows, SEL_CHUNK), F32), pltpu.VMEM((rows, SEL_CHUNK), BF16),
                            pltpu.VMEM((rows, 128), F32), pltpu.VMEM((rows, 128), F32),
                            pltpu.VMEM((rows, 128), F32), pltpu.VMEM((rows, HEAD_DIM), F32)]),
        out_shape=jax.ShapeDtypeStruct((B, S, H * HEAD_DIM), BF16),
        compiler_params=_cparams(3),
        name="nsa_sel_win",
    )(bits, coefs, q, k128, k128, kv, kv, nsel, ocmp, gates)


def _out_proj_kernel(o_ref, w_ref, x_ref, y_ref):
    y_ref[0] = x_ref[0] + _dot(o_ref[0], w_ref[...])


def _out_proj(o, w, x, tm=512):
    B, S, D = x.shape
    return pl.pallas_call(
        _out_proj_kernel,
        grid=(B, S // tm),
        in_specs=[pl.BlockSpec((1, tm, o.shape[2]), lambda b, i: (b, i, 0)),
                  pl.BlockSpec(w.shape, lambda b, i: (0, 0)),
                  pl.BlockSpec((1, tm, D), lambda b, i: (b, i, 0))],
        out_specs=pl.BlockSpec((1, tm, D), lambda b, i: (b, i, 0)),
        out_shape=jax.ShapeDtypeStruct((B, S, D), F32),
        compiler_params=_cparams(2),
        name="out_proj",
    )(o, w, x)


def _diff_proj_kernel(x_ref, g_ref, w_ref, eq_ref, ek_ref, q_ref, k_ref, v_ref):
    h = _rms(x_ref[0], g_ref[...]).astype(BF16)
    nqk = DIFF_HEADS * DIFF_V
    for hd in range(DIFF_HEADS):
        lo, mid, hi = DIFF_QK * hd, DIFF_QK * hd + DIFF_V, DIFF_QK * (hd + 1)
        q_ref[0, :, lo:mid] = _dot(h, w_ref[:, DIFF_V * hd:DIFF_V * (hd + 1)]).astype(BF16)
        q_ref[0, :, mid:hi] = eq_ref[:, DIFF_V * hd:DIFF_V * (hd + 1)]
        k_ref[0, :, lo:mid] = _dot(h, w_ref[:, nqk + DIFF_V * hd:nqk + DIFF_V * (hd + 1)]).astype(BF16)
        k_ref[0, :, mid:hi] = ek_ref[:, DIFF_V * hd:DIFF_V * (hd + 1)]
    for c in range(nqk // 512):
        v_ref[0, :, 512 * c:512 * (c + 1)] = _dot(h, w_ref[:, 2 * nqk + 512 * c:2 * nqk + 512 * (c + 1)]).astype(BF16)


def _diff_proj(x, g, w, eq, ek):
    B, S, D = x.shape
    tm = POS_PERIOD
    nqk = DIFF_HEADS * DIFF_QK
    nv = DIFF_HEADS * DIFF_V
    row = lambda b, i: (b, i, 0)
    const = lambda b, i: (0, 0)
    return pl.pallas_call(
        _diff_proj_kernel,
        grid=(B, S // tm),
        in_specs=[pl.BlockSpec((1, tm, D), row),
                  pl.BlockSpec((1, D), const),
                  pl.BlockSpec(w.shape, const),
                  pl.BlockSpec(eq.shape, const),
                  pl.BlockSpec(ek.shape, const)],
        out_specs=[pl.BlockSpec((1, tm, nqk), row), pl.BlockSpec((1, tm, nqk), row),
                   pl.BlockSpec((1, tm, nv), row)],
        out_shape=[jax.ShapeDtypeStruct((B, S, nqk), BF16), jax.ShapeDtypeStruct((B, S, nqk), BF16),
                   jax.ShapeDtypeStruct((B, S, nv), BF16)],
        compiler_params=_cparams(2),
        name="diff_proj",
    )(x, g, w, eq, ek)


def _diff_attn_kernel(coef_ref, q_ref, k_ref, v_ref, lam_ref, sg_ref, o_ref,
                      *scratch, tq, lambda_init):
    s_refs = scratch[0:DIFF_GROUP]
    p_refs = scratch[DIFF_GROUP:2 * DIFF_GROUP]
    al_refs = scratch[2 * DIFF_GROUP:3 * DIFF_GROUP]
    m_ref, l_ref, acc_ref = scratch[3 * DIFF_GROUP:]
    h = pl.program_id(1)
    i = pl.program_id(2)
    coef = coef_ref[h]
    q = q_ref[0]
    lane = lax.broadcasted_iota(jnp.int32, q.shape, 1)
    zero = jnp.zeros_like(q)
    qs = (jnp.where((lane < HEAD_DIM) | (lane >= DIFF_V), q, zero), jnp.where(lane >= HEAD_DIM, q, zero))

    m_ref[...] = jnp.full(m_ref.shape, MASK_VALUE, F32)
    l_ref[...] = jnp.zeros(l_ref.shape, F32)
    acc_ref[...] = jnp.zeros(acc_ref.shape, F32)

    def qk(j, s_ref):
        kt = k_ref[0, pl.ds(pl.multiple_of(j * tq, tq), tq), :]
        for c in range(2):
            s_ref[c] = _dot_nt(qs[c], kt)

    def softmax(j, s_ref, p_ref, alpha_ref, diag):
        base = coef * ((i - j) * tq).astype(F32)
        for c in range(2):
            for r in range(tq // ROW_BLOCK):
                rows = slice(r * ROW_BLOCK, (r + 1) * ROW_BLOCK)
                s = s_ref[c, rows, :]
                if diag:
                    col = lax.broadcasted_iota(jnp.int32, s.shape, 1)
                    row = r * ROW_BLOCK + lax.broadcasted_iota(jnp.int32, s.shape, 0)
                    s = jnp.where(col <= row, s, MASK_VALUE)
                m_old = m_ref[c, rows, :]
                m_new = jnp.maximum(m_old, jnp.max(s, axis=1, keepdims=True) - base)
                p = jnp.exp2(s - pltpu.repeat(m_new + base, tq // 128, axis=1))
                alpha = jnp.exp2(m_old - m_new)
                m_ref[c, rows, :] = m_new
                l_ref[c, rows, :] = (alpha * l_ref[c, rows, :]
                                     + sum(p[:, 128 * t:128 * (t + 1)] for t in range(tq // 128)))
                alpha_ref[c, rows, :] = alpha
                p_ref[c, rows, :] = p.astype(BF16)

    def pv(j, p_ref, alpha_ref):
        vt = v_ref[0, pl.ds(pl.multiple_of(j * tq, tq), tq), :]
        for c in range(2):
            acc_ref[c] = alpha_ref[c] * acc_ref[c] + _dot(p_ref[c], vt)

    def tile(j, r, diag):
        qk(j, s_refs[r])
        softmax(j, s_refs[r], p_refs[r], al_refs[r], diag)
        pv(j, p_refs[r], al_refs[r])

    tile(i, 0, True)
    group = len(s_refs)

    def body(jj, carry):
        for r in range(group):
            tile(group * jj + r, r, False)
        return carry

    lax.fori_loop(0, i // group, body, 0)

    def rest(j, carry):
        tile(j, 0, False)
        return carry

    lax.fori_loop((i // group) * group, i, rest, 0)

    lv = lam_ref[...]
    lam = (jnp.exp(jnp.sum(lv[0:1] * lv[1:2], axis=1, keepdims=True))
           - jnp.exp(jnp.sum(lv[2:3] * lv[3:4], axis=1, keepdims=True)) + lambda_init)
    l0 = jnp.sum(l_ref[0], axis=1, keepdims=True)
    l1 = jnp.sum(l_ref[1], axis=1, keepdims=True)
    o = (acc_ref[0] * (1.0 / jnp.maximum(l0, 1e-30))
         - lam * (acc_ref[1] * (1.0 / jnp.maximum(l1, 1e-30))))
    o = _rms(o, sg_ref[...]) * (1.0 - lambda_init)
    o_ref[0] = o.astype(BF16)


def _diff_attn(coefs, qx, kx, v, lam_vecs, subln_g, lambda_init):
    B, S, _ = v.shape
    H = DIFF_HEADS
    tq = POS_PERIOD
    kern = functools.partial(_diff_attn_kernel, tq=tq, lambda_init=lambda_init)
    return pl.pallas_call(
        kern,
        grid_spec=pltpu.PrefetchScalarGridSpec(
            num_scalar_prefetch=1,
            grid=(B, H, S // tq),
            in_specs=[pl.BlockSpec((1, tq, DIFF_QK), lambda b, h, i, sl: (b, i, h)),
                      pl.BlockSpec((1, S, DIFF_QK), lambda b, h, i, sl: (b, 0, h)),
                      pl.BlockSpec((1, S, DIFF_V), lambda b, h, i, sl: (b, 0, h)),
                      pl.BlockSpec((4, HEAD_DIM), lambda b, h, i, sl: (0, 0)),
                      pl.BlockSpec((1, DIFF_V), lambda b, h, i, sl: (0, 0))],
            out_specs=pl.BlockSpec((1, tq, DIFF_V), lambda b, h, i, sl: (b, i, h)),
            scratch_shapes=([pltpu.VMEM((2, tq, tq), F32)] * DIFF_GROUP
                            + [pltpu.VMEM((2, tq, tq), BF16)] * DIFF_GROUP
                            + [pltpu.VMEM((2, tq, 128), F32)] * DIFF_GROUP
                            + [pltpu.VMEM((2, tq, 128), F32), pltpu.VMEM((2, tq, 128), F32),
                               pltpu.VMEM((2, tq, DIFF_V), F32)])),
        out_shape=jax.ShapeDtypeStruct((B, S, H * DIFF_V), BF16),
        compiler_params=_cparams(3),
        name="diff_attn",
    )(coefs, qx, kx, v, lam_vecs, subln_g)


def _ffn_kernel(x_ref, g_ref, wup_ref, cw_ref, cb_ref, wdn_ref, fg_ref, y_ref, ug_ref, uv_ref,
                *, tm, final_norm):
    i = pl.program_id(1)

    @pl.when(i == 0)
    def _():
        ug_ref[0:8, :] = jnp.zeros((8, D_FF), F32)
        uv_ref[0:8, :] = jnp.zeros((8, D_FF), F32)

    x = x_ref[0]
    h = _rms(x, g_ref[...]).astype(BF16)
    acc = jnp.zeros((tm, D_MODEL), F32)
    for c in range(D_FF // FF_CHUNK):
        halves = []
        for u_ref, col0 in ((ug_ref, FF_CHUNK * c), (uv_ref, D_FF + FF_CHUNK * c)):
            cols = slice(FF_CHUNK * c, FF_CHUNK * (c + 1))
            u_ref[8:8 + tm, cols] = _dot(h, wup_ref[:, col0:col0 + FF_CHUNK])
            w = cw_ref[:, col0:col0 + FF_CHUNK]
            y = (u_ref[6:6 + tm, cols] * w[0:1] + u_ref[7:7 + tm, cols] * w[1:2]
                 + u_ref[8:8 + tm, cols] * w[2:3] + cb_ref[:, col0:col0 + FF_CHUNK])
            u_ref[0:8, cols] = u_ref[tm:tm + 8, cols]
            halves.append(y)
        act = (jax.nn.silu(halves[0]) * halves[1]).astype(BF16)
        acc = acc + _dot(act, wdn_ref[FF_CHUNK * c:FF_CHUNK * (c + 1), :])
    out = x + acc
    if final_norm:
        out = _rms(out, fg_ref[...])
    y_ref[0] = out


def _ffn(x, g, w_up, conv_w, conv_b, w_down, final_g, final_norm, tm=256):
    B, S, D = x.shape
    kern = functools.partial(_ffn_kernel, tm=tm, final_norm=final_norm)
    const = lambda b, i: (0, 0)
    return pl.pallas_call(
        kern,
        grid=(B, S // tm),
        in_specs=[pl.BlockSpec((1, tm, D), lambda b, i: (b, i, 0)),
                  pl.BlockSpec((1, D), const),
                  pl.BlockSpec(w_up.shape, const, pipeline_mode=pl.Buffered(1)),
                  pl.BlockSpec(conv_w.shape, const),
                  pl.BlockSpec(conv_b.shape, const),
                  pl.BlockSpec(w_down.shape, const, pipeline_mode=pl.Buffered(1)),
                  pl.BlockSpec((1, D), const)],
        out_specs=pl.BlockSpec((1, tm, D), lambda b, i: (b, i, 0)),
        out_shape=jax.ShapeDtypeStruct((B, S, D), F32),
        scratch_shapes=[pltpu.VMEM((tm + 8, D_FF), F32), pltpu.VMEM((tm + 8, D_FF), F32)],
        compiler_params=_cparams(2),
        name="conv_ffn",
    )(x, g, w_up, conv_w, conv_b, w_down, final_g)


def _alibi_slopes(n):
    return jnp.exp2(-8.0 * jnp.arange(1, n + 1, dtype=F32) / n)


def _nsa_weight(w_in):
    gates = w_in[:, NSA_QKV_COLS:].reshape(D_MODEL, NSA_GROUPS, NSA_HPG, 3)
    gates = gates.transpose(0, 1, 3, 2).reshape(D_MODEL, NSA_GROUPS, 3 * NSA_HPG)
    gates = jnp.pad(gates, ((0, 0), (0, 0), (0, NSA_GATE_STRIDE - 3 * NSA_HPG)))
    gates = gates.reshape(D_MODEL, NSA_GROUPS * NSA_GATE_STRIDE)
    gates = jnp.pad(gates, ((0, 0), (0, NSA_GATE_PAD - NSA_GROUPS * NSA_GATE_STRIDE)))

    def widen(cols):
        w = w_in[:, cols].reshape(D_MODEL, -1, HEAD_DIM)
        return jnp.pad(w, ((0, 0), (0, 0), (0, NSA_ROW - HEAD_DIM))).reshape(D_MODEL, -1)

    nq, nk = NSA_HEADS * HEAD_DIM, NSA_GROUPS * HEAD_DIM
    kc, vc, ks, vs, kw, vw = [slice(nq + nk * b, nq + nk * (b + 1)) for b in range(6)]
    return jnp.concatenate([widen(slice(0, nq)), widen(ks), widen(kw),
                            w_in[:, kc], w_in[:, vc], w_in[:, vs], w_in[:, vw], gates], axis=1).astype(BF16)


def _coef_pieces(coefs):
    c_hi = coefs.astype(BF16)
    c_mid = (coefs - c_hi.astype(F32)).astype(BF16)
    c_lo = (coefs - c_hi.astype(F32) - c_mid.astype(F32)).astype(BF16)
    return jnp.stack([c_hi, c_hi, c_mid, c_mid, c_lo, c_lo], axis=1)


def _nsa_columns(coefs, nc):
    def place(cols, lane0):
        return jnp.pad(cols.astype(F32), ((0, 0), (lane0, NSA_ROW - lane0 - cols.shape[1])))

    q_cols = place(_coef_pieces(coefs), COL_COEF)
    pos = jnp.arange(SEL_CHUNK)
    k_pos = jnp.stack([pos - pos % 2, pos % 2] * 3, axis=1)
    k_mask = jnp.where(pos[:, None] // SEL_BLOCK == jnp.arange(BLOCKS_PER_CHUNK)[None, :], MASK_BIG, 0.0)
    k_cols = place(k_pos, COL_COEF) + place(k_mask, COL_MASK)
    j = jnp.arange(nc)
    c_pos = jnp.stack([4 * CMP_STRIDE * (j // 4), CMP_STRIDE * (j % 4)] * 3, axis=1)
    cmp_cols = jnp.stack([place(c_pos, COL_COEF), jnp.zeros((nc, NSA_ROW), F32)])
    return q_cols, k_cols, cmp_cols


def _nsa_layer(x, norm_g, w_in, k_pe, k_w1, k_w2, v_pe, v_w1, v_w2, w_out):
    B, S, _ = x.shape
    nc = S // CMP_STRIDE
    coefs = _alibi_slopes(NSA_HEADS) * LOG2E
    q_cols, k_cols, cmp_cols = _nsa_columns(coefs, nc)
    q, k128, kv, gates = _nsa_proj(x, norm_g[None], _nsa_weight(w_in), q_cols, k_cols)

    xc = kv[0:2].reshape(2, B, NSA_GROUPS, nc, CMP_STRIDE * HEAD_DIM)
    pe8 = jnp.broadcast_to(jnp.stack([k_pe, v_pe]).reshape(2, 1, CMP_BLOCK * HEAD_DIM),
                           (2, 8, CMP_BLOCK * HEAD_DIM)).astype(BF16)
    w2 = jnp.pad(jnp.stack([k_w2, v_w2]), ((0, 0), (0, 0), (0, NSA_ROW - HEAD_DIM))).astype(BF16)
    cmp = _compress(xc, pe8, jnp.stack([k_w1, v_w1]).astype(BF16), w2, cmp_cols)
    kcmp, vcmp_t = cmp[0], cmp[1][..., :HEAD_DIM].transpose(0, 1, 3, 2)

    nsb = S // SEL_BLOCK
    ratio = SEL_BLOCK // CMP_STRIDE
    sb = jnp.arange(nsb)[:, None]
    cb = jnp.arange(nc)[None, :]
    pool_t = ((cb >= ratio * sb - 1) & (cb <= ratio * sb + ratio - 1)).astype(BF16)
    grp = (jnp.arange(nsb)[:, None] // BLOCKS_PER_CHUNK == jnp.arange(128)[None, :]).astype(BF16)
    ocmp, nsel, bits = _cmp_topk(q, kcmp, vcmp_t, pool_t, grp)
    word = bits[:, 0, 0] | (bits[:, 0, 1] << 16)
    o = _sel_win(word, coefs, q, k128, kv, nsel, ocmp, gates)
    return _out_proj(o, w_out.astype(BF16), x)


def _alibi_columns(coefs, period):
    pieces = _coef_pieces(coefs)
    pos = jnp.arange(period)
    parts = jnp.stack([pos - pos % 2, pos % 2] * 3, axis=1).astype(BF16)
    n = coefs.shape[0]
    pieces_b = jnp.broadcast_to(pieces[None], (period, n, 6))
    parts_b = jnp.broadcast_to(parts[:, None], (period, n, 6))
    pad = jnp.zeros((period, n, DIFF_V - 12), BF16)
    eq = jnp.concatenate([pieces_b, parts_b, pad], axis=2).reshape(period, n * DIFF_V)
    ek = jnp.concatenate([parts_b, -pieces_b, pad], axis=2).reshape(period, n * DIFF_V)
    return eq, ek


def _diff_layer(x, norm_g, w_in, lq1, lk1, lq2, lk2, subln_g, w_out, lambda_init):
    nqk = DIFF_HEADS * DIFF_V
    scale = jnp.concatenate([jnp.full((nqk,), HEAD_DIM ** -0.5 * LOG2E, F32), jnp.ones((2 * nqk,), F32)])
    coefs = _alibi_slopes(DIFF_HEADS) * LOG2E
    eq, ek = _alibi_columns(coefs, POS_PERIOD)
    qx, kx, v = _diff_proj(x, norm_g[None], (w_in * scale[None, :]).astype(BF16), eq, ek)
    lam_vecs = jnp.stack([lq1, lk1, lq2, lk2]).astype(F32)
    o = _diff_attn(coefs, qx, kx, v, lam_vecs, subln_g[None].astype(F32), lambda_init)
    return _out_proj(o, w_out.astype(BF16), x)


def kernel(x, norm_mix_g, norm_ffn_g, final_norm_g, nsa_w_in, nsa_cmp_k_pe, nsa_cmp_k_w1, nsa_cmp_k_w2, nsa_cmp_v_pe, nsa_cmp_v_w1, nsa_cmp_v_w2, nsa_w_out, diff_w_in, diff_lam_q1, diff_lam_k1, diff_lam_q2, diff_lam_k2, diff_subln_g, diff_w_out, ffn_w_up, ffn_conv_w, ffn_conv_b, ffn_w_down):
    depth = norm_mix_g.shape[0]
    for i in range(depth):
        j = i // 2
        if i % 2 == 0:
            x = _nsa_layer(x, norm_mix_g[i], nsa_w_in[j], nsa_cmp_k_pe[j], nsa_cmp_k_w1[j], nsa_cmp_k_w2[j],
                           nsa_cmp_v_pe[j], nsa_cmp_v_w1[j], nsa_cmp_v_w2[j], nsa_w_out[j])
        else:
            lambda_init = 0.8 - 0.6 * math.exp(-0.3 * i)
            x = _diff_layer(x, norm_mix_g[i], diff_w_in[j], diff_lam_q1[j], diff_lam_k1[j], diff_lam_q2[j],
                            diff_lam_k2[j], diff_subln_g[j], diff_w_out[j], lambda_init)
        x = _ffn(x, norm_ffn_g[i][None], ffn_w_up[i].astype(BF16), ffn_conv_w[i], ffn_conv_b[i][None],
                 ffn_w_down[i].astype(BF16), final_norm_g[None], final_norm=(i == depth - 1))
    return x
```

```python
import functools
import math

import jax
import jax.numpy as jnp
from jax import lax
from jax.experimental import pallas as pl
from jax.experimental.pallas import tpu as pltpu

F32 = jnp.float32
BF16 = jnp.bfloat16

D_MODEL = 1024
EPS = 1e-6
MASK_VALUE = -1e30
FORCED_SCORE = 1e6

NSA_HEADS = 16
NSA_GROUPS = 4
NSA_HPG = 4
HEAD_DIM = 64
CMP_STRIDE = 16
CMP_BLOCK = 32
CMP_HIDDEN = 256
SEL_BLOCK = 64
SEL_TOP_N = 16
WINDOW = 512
NSA_QKV_COLS = NSA_HEADS * HEAD_DIM + 6 * NSA_GROUPS * HEAD_DIM
NSA_GATE_PAD = 128
NSA_GATE_STRIDE = 16

DIFF_HEADS = 8
DIFF_V = 128
DIFF_QK = 256
POS_PERIOD = 512
ROW_BLOCK = 32
DIFF_GROUP = 3
LOG2E = 1.4426950408889634
UNDERFLOW_LOG2 = 160.0
NORM_SLACK = 1.01

D_FF = 2816
FF_CHUNK = 256

SEL_CHUNK = 512
BLOCKS_PER_CHUNK = SEL_CHUNK // SEL_BLOCK
MASK_BIG = -(2.0 ** 100)
NO_KEY_FLOOR = -1e29
NSA_ROW = 128
COL_COEF = 64
COL_MASK = 70
CMP_CHUNK = 256
NSA_TQ = 256

VMEM_LIMIT = 56 * 1024 * 1024


def _cparams(n_axes):
    return pltpu.CompilerParams(dimension_semantics=("arbitrary",) * n_axes,
                                vmem_limit_bytes=VMEM_LIMIT)


def _rms(x, g):
    return x * lax.rsqrt(jnp.mean(x * x, axis=-1, keepdims=True) + EPS) * g


def _dot(a, b):
    return jnp.dot(a, b, preferred_element_type=F32)


def _lane_repeat(x, n):
    return jnp.concatenate([x] * n, axis=1)


def _dot_nt(a, b):
    return lax.dot_general(a, b, (((1,), (1,)), ((), ())), preferred_element_type=F32)


def _nsa_proj_kernel(x_ref, g_ref, w_ref, qc_ref, kc_ref, q_ref, k_ref, kv_ref, gate_ref, n_ref):
    h = _rms(x_ref[0], g_ref[...]).astype(BF16)
    tm = h.shape[0]
    lane = lax.broadcasted_iota(jnp.int32, (8, 128), 1)
    norms = jnp.zeros((8, 128), F32)
    col = 0
    for pair in range(NSA_HEADS // 2):
        pc = _dot(h, w_ref[:, col:col + 2 * NSA_ROW]) * (HEAD_DIM ** -0.5 * LOG2E)
        for s in range(2):
            hd = 2 * pair + s
            qh = pc[:, NSA_ROW * s:NSA_ROW * (s + 1)].astype(BF16)
            q_ref[0, hd] = qh + qc_ref[hd:hd + 1, :].astype(BF16)
            for sub in range(tm // NSA_TQ):
                part = _max_row_norm_sq(qh[sub * NSA_TQ:(sub + 1) * NSA_TQ])
                norms = jnp.where(lane == sub * NSA_HEADS + hd, part, norms)
        col += 2 * NSA_ROW
    for br in range(2):
        for pair in range(NSA_GROUPS // 2):
            pc = _dot(h, w_ref[:, col:col + 2 * NSA_ROW])
            for s in range(2):
                piece = pc[:, NSA_ROW * s:NSA_ROW * (s + 1)].astype(BF16)
                if br == 0:
                    k_ref[br, 0, 2 * pair + s] = piece + kc_ref[...].astype(BF16)
                    norms = jnp.where(lane == (tm // NSA_TQ) * NSA_HEADS + 2 * pair + s,
                                      _max_row_norm_sq(piece), norms)
                else:
                    k_ref[br, 0, 2 * pair + s] = piece
            col += 2 * NSA_ROW
    n_ref[0, 0] = norms
    for br in range(4):
        pc = _dot(h, w_ref[:, col:col + 256])
        for g in range(NSA_GROUPS):
            kv_ref[br, 0, g] = pc[:, 64 * g:64 * (g + 1)].astype(BF16)
        col += 256
    pg = _dot(h, w_ref[:, col:col + NSA_GATE_PAD])
    for g in range(NSA_GROUPS):
        gate_ref[0, g] = jax.nn.sigmoid(pg[:, NSA_GATE_STRIDE * g:NSA_GATE_STRIDE * (g + 1)])


def _nsa_proj(x, g, w, q_cols, k_cols):
    B, S, D = x.shape
    tm = SEL_CHUNK
    n = w.shape[1]
    return pl.pallas_call(
        _nsa_proj_kernel,
        grid=(B, S // tm),
        in_specs=[pl.BlockSpec((1, tm, D), lambda b, i: (b, i, 0)),
                  pl.BlockSpec((1, D), lambda b, i: (0, 0)),
                  pl.BlockSpec((D, n), lambda b, i: (0, 0)),
                  pl.BlockSpec(q_cols.shape, lambda b, i: (0, 0)),
                  pl.BlockSpec(k_cols.shape, lambda b, i: (0, 0))],
        out_specs=[pl.BlockSpec((1, NSA_HEADS, tm, NSA_ROW), lambda b, i: (b, 0, i, 0)),
                   pl.BlockSpec((2, 1, NSA_GROUPS, tm, NSA_ROW), lambda b, i: (0, b, 0, i, 0)),
                   pl.BlockSpec((4, 1, NSA_GROUPS, tm, HEAD_DIM), lambda b, i: (0, b, 0, i, 0)),
                   pl.BlockSpec((1, NSA_GROUPS, tm, NSA_GATE_STRIDE), lambda b, i: (b, 0, i, 0)),
                   pl.BlockSpec((1, 1, 8, 128), lambda b, i: (b, i, 0, 0))],
        out_shape=[jax.ShapeDtypeStruct((B, NSA_HEADS, S, NSA_ROW), BF16),
                   jax.ShapeDtypeStruct((2, B, NSA_GROUPS, S, NSA_ROW), BF16),
                   jax.ShapeDtypeStruct((4, B, NSA_GROUPS, S, HEAD_DIM), BF16),
                   jax.ShapeDtypeStruct((B, NSA_GROUPS, S, NSA_GATE_STRIDE), F32),
                   jax.ShapeDtypeStruct((B, S // tm, 8, 128), F32)],
        compiler_params=_cparams(2),
        name="nsa_proj",
    )(x, g, w, q_cols, k_cols)


def _compress_kernel(x_ref, pe_ref, w1_ref, w2_ref, cols_ref, o_ref, shift_ref):
    nc = x_ref.shape[3]
    half = CMP_STRIDE * HEAD_DIM
    x = x_ref[0, 0, 0]
    first = _dot(x, w1_ref[0, :half, :])
    second = _dot(x, w1_ref[0, half:, :])
    shift_ref[0:nc, :] = second
    shift_ref[nc:nc + 8, :] = jnp.zeros((8, CMP_HIDDEN), F32)
    pe_term = _dot(pe_ref[0], w1_ref[0])[0:1, :]
    pre = first + shift_ref[pl.ds(1, nc), :] + pe_term
    hid = jax.nn.gelu(pre, approximate=True)
    o_ref[0, 0, 0] = (_dot(hid.astype(BF16), w2_ref[0]) + cols_ref[0]).astype(BF16)


def _compress(xc, pe8, w1, w2, cols):
    _, B, G, nc, width = xc.shape
    return pl.pallas_call(
        _compress_kernel,
        grid=(2, B, G),
        in_specs=[pl.BlockSpec((1, 1, 1, nc, width), lambda a, b, g: (a, b, g, 0, 0)),
                  pl.BlockSpec((1, 8, CMP_BLOCK * HEAD_DIM), lambda a, b, g: (a, 0, 0)),
                  pl.BlockSpec((1, CMP_BLOCK * HEAD_DIM, CMP_HIDDEN), lambda a, b, g: (a, 0, 0)),
                  pl.BlockSpec((1, CMP_HIDDEN, NSA_ROW), lambda a, b, g: (a, 0, 0)),
                  pl.BlockSpec((1, nc, NSA_ROW), lambda a, b, g: (a, 0, 0))],
        out_specs=pl.BlockSpec((1, 1, 1, nc, NSA_ROW), lambda a, b, g: (a, b, g, 0, 0)),
        out_shape=jax.ShapeDtypeStruct((2, B, G, nc, NSA_ROW), BF16),
        scratch_shapes=[pltpu.VMEM((nc + 8, CMP_HIDDEN), F32)],
        compiler_params=_cparams(3),
        name="nsa_compress",
    )(xc, pe8, w1, w2, cols)


def _cmp_topk_kernel(q_ref, kc_ref, vct_ref, pool_ref, grp_ref, ocmp_ref, nsel_ref, bits_ref, *, tq):
    i = pl.program_id(2)
    ncp = kc_ref.shape[2]
    nsel_blocks = pool_ref.shape[0]
    t0 = i * tq
    ratio = SEL_BLOCK // CMP_STRIDE

    def visible_prefix(k):
        rows = CMP_CHUNK * k
        nsb = rows // ratio
        kc = kc_ref[0, 0, 0:rows, :]
        vct = vct_ref[0, 0, :, 0:rows]
        tok = t0 + lax.broadcasted_iota(jnp.int32, (rows, tq), 1)
        cmp_end = lax.broadcasted_iota(jnp.int32, (rows, tq), 0) * CMP_STRIDE + (CMP_BLOCK - 1)
        mask = tok >= cmp_end

        p_grp = jnp.zeros((rows, tq), F32)
        for p in range(NSA_HPG):
            s = jnp.where(mask, _dot_nt(kc, q_ref[0, p]), MASK_VALUE)
            m = jnp.maximum(jnp.max(s, axis=0, keepdims=True), NO_KEY_FLOOR)
            e = jnp.exp2(s - m)
            r = 1.0 / jnp.maximum(jnp.sum(e, axis=0, keepdims=True), 1e-30)
            p_grp = p_grp + e * r
            ocmp_ref[0, p] = (_dot(vct, e.astype(BF16)) * r).T

        hi = p_grp.astype(BF16)
        r1 = p_grp - hi.astype(F32)
        mid = r1.astype(BF16)
        lo = (r1 - mid.astype(F32)).astype(BF16)
        pool = pool_ref[0:nsb, 0:rows]
        imp = _dot(pool, hi) + _dot(pool, mid) + _dot(pool, lo)

        blk = lax.broadcasted_iota(jnp.int32, (nsb, tq), 0)
        cur = (t0 + lax.broadcasted_iota(jnp.int32, (nsb, tq), 1)) // SEL_BLOCK
        valid = blk <= cur
        forced = (blk == 0) | (valid & (blk >= cur - 1))
        work = jnp.where(forced, FORCED_SCORE, jnp.where(valid, imp, -1.0))
        blk_f = blk.astype(F32)
        sel = jnp.zeros((nsb, tq), F32)
        for _ in range(min(SEL_TOP_N, nsb)):
            mx = jnp.max(work, axis=0, keepdims=True)
            first = jnp.min(jnp.where(work == mx, blk_f, float(nsb)), axis=0, keepdims=True)
            pick = blk_f == first
            sel = jnp.where(pick, 1.0, sel)
            work = jnp.where(pick, -jnp.inf, work)
        sel = jnp.where(valid, sel, 0.0)
        if nsb < nsel_blocks:
            sel = jnp.concatenate([sel, jnp.zeros((nsel_blocks - nsb, tq), F32)], axis=0)

        sel_t = sel.T
        nsel_ref[0, 0] = (1.0 - sel_t).astype(BF16)

        any_blk = jnp.max(sel_t, axis=0, keepdims=True)
        cnt = _dot(jnp.broadcast_to(any_blk, (8, nsel_blocks)).astype(BF16), grp_ref[...])[0:1, :]
        lane = lax.broadcasted_iota(jnp.int32, (1, 128), 1)
        flag = cnt > 0.0
        bit = jnp.left_shift(1, lane & 15).astype(F32)
        w_lo = jnp.where(flag & (lane < 16), bit, 0.0)
        w_hi = jnp.where(flag & (lane >= 16) & (lane < 32), bit, 0.0)
        lo_word = jnp.sum(w_lo, axis=1, keepdims=True)
        hi_word = jnp.sum(w_hi, axis=1, keepdims=True)
        bits_ref[0] = jnp.where(lane == 0, lo_word, jnp.where(lane == 1, hi_word, 0.0)).astype(jnp.int32)

    n_chunks = ncp // CMP_CHUNK
    last_visible = (t0 + tq - CMP_BLOCK) // CMP_STRIDE
    lax.switch(jnp.clip(last_visible // CMP_CHUNK, 0, n_chunks - 1),
               [functools.partial(visible_prefix, k) for k in range(1, n_chunks + 1)])


def _cmp_topk(q, kcmp, vcmp_t, pool_t, grp, tq=NSA_TQ):
    B, H, S, _ = q.shape
    G = NSA_GROUPS
    ncp = kcmp.shape[2]
    nsb = pool_t.shape[0]
    nq = S // tq
    kern = functools.partial(_cmp_topk_kernel, tq=tq)
    return pl.pallas_call(
        kern,
        grid=(B, G, nq),
        in_specs=[pl.BlockSpec((1, NSA_HPG, tq, NSA_ROW), lambda b, g, i: (b, g, i, 0)),
                  pl.BlockSpec((1, 1, ncp, NSA_ROW), lambda b, g, i: (b, g, 0, 0)),
                  pl.BlockSpec((1, 1, HEAD_DIM, ncp), lambda b, g, i: (b, g, 0, 0)),
                  pl.BlockSpec((nsb, ncp), lambda b, g, i: (0, 0)),
                  pl.BlockSpec((nsb, 128), lambda b, g, i: (0, 0))],
        out_specs=[pl.BlockSpec((1, NSA_HPG, tq, HEAD_DIM), lambda b, g, i: (b, g, i, 0)),
                   pl.BlockSpec((1, 1, tq, nsb), lambda b, g, i: (b, g, i, 0)),
                   pl.BlockSpec((1, 1, 128), lambda b, g, i: ((b * G + g) * nq + i, 0, 0))],
        out_shape=[jax.ShapeDtypeStruct((B, H, S, HEAD_DIM), F32),
                   jax.ShapeDtypeStruct((B, G, S, nsb), BF16),
                   jax.ShapeDtypeStruct((B * G * nq, 1, 128), jnp.int32)],
        compiler_params=_cparams(3),
        name="nsa_cmp_topk",
    )(q, kcmp, vcmp_t, pool_t, grp)


def _sel_win_kernel(bits_ref, coef_ref, norm_ref, q_ref, ks_ref, kw_ref, vs_ref, vw_ref, nsel_ref,
                    ocmp_ref, gate_ref, o_ref, s_ref, p_ref, alpha_ref, m_ref, l_ref, acc_ref, *, tq):
    b = pl.program_id(0)
    g = pl.program_id(1)
    i = pl.program_id(2)
    nq = pl.num_programs(2)
    nsb = nsel_ref.shape[3]
    rows = NSA_HPG * tq
    t0 = i * tq
    q4 = q_ref[0].reshape(rows, NSA_ROW)
    word = bits_ref[((b * NSA_GROUPS + g) * nq + i)]
    coefs = [coef_ref[NSA_HPG * g + p] for p in range(NSA_HPG)]
    coef_rows = jnp.concatenate([jnp.full((tq, 128), c, F32) for c in coefs], axis=0)
    diag_chunk = t0 // SEL_CHUNK

    n_chunks = nq * tq // SEL_CHUNK
    q_row = (b * nq + i) * NSA_HEADS + NSA_HPG * g
    k_off = pl.num_programs(0) * nq * NSA_HEADS
    qn = functools.reduce(jnp.maximum, [norm_ref[q_row + p] for p in range(NSA_HPG)])
    coef_min = functools.reduce(jnp.minimum, coefs)
    kn_diag = norm_ref[k_off + (b * n_chunks + diag_chunk) * NSA_GROUPS + g]

    def prune(c, w):
        gap = (t0 - (c * SEL_CHUNK + SEL_CHUNK - 1)).astype(F32)
        kn = norm_ref[k_off + (b * n_chunks + c) * NSA_GROUPS + g]
        bound = qn * (kn + kn_diag) * NORM_SLACK + 1.0 - coef_min * gap
        return jnp.where(bound < -UNDERFLOW_LOG2, w & ~jnp.left_shift(1, c), w)

    word = lax.fori_loop(0, diag_chunk, prune, word)

    m_ref[...] = jnp.full(m_ref.shape, MASK_VALUE, F32)
    l_ref[...] = jnp.zeros(l_ref.shape, F32)
    acc_ref[...] = jnp.zeros(acc_ref.shape, F32)
    nsel = nsel_ref[0, 0].astype(F32)
    lane = lax.broadcasted_iota(jnp.int32, (tq, NSA_ROW), 1)
    flag_lanes = (lane >= COL_MASK) & (lane < COL_MASK + BLOCKS_PER_CHUNK)
    diag_chunk = t0 // SEL_CHUNK

    def chunk_step(c, causal):
        k0 = pl.multiple_of(c * SEL_CHUNK, SEL_CHUNK)
        kc = ks_ref[0, 0, pl.ds(k0, SEL_CHUNK), :]
        vc = vs_ref[0, 0, pl.ds(k0, SEL_CHUNK), :]
        shift = (COL_MASK + nsb - BLOCKS_PER_CHUNK * c) % nsb
        flags = jnp.where(flag_lanes, pltpu.roll(nsel, shift, 1)[:, :NSA_ROW], 0.0).astype(BF16)
        s_ref[...] = _dot_nt(q4 + jnp.tile(flags, (NSA_HPG, 1)), kc)
        base = coef_rows * (k0 - t0).astype(F32)
        for r in range(rows // ROW_BLOCK):
            rr = slice(r * ROW_BLOCK, (r + 1) * ROW_BLOCK)
            s = s_ref[rr, :]
            if causal:
                col = lax.broadcasted_iota(jnp.int32, s.shape, 1)
                row = (r * ROW_BLOCK) % tq + lax.broadcasted_iota(jnp.int32, s.shape, 0)
                s = jnp.where(col - row <= t0 - k0, s, MASK_VALUE)
            m_old = m_ref[rr, :]
            m_new = jnp.maximum(m_old, jnp.max(s, axis=1, keepdims=True) + base[rr])
            p = jnp.exp2(s - _lane_repeat(m_new - base[rr], SEL_CHUNK // 128))
            alpha = jnp.exp2(m_old - m_new)
            m_ref[rr, :] = m_new
            l_ref[rr, :] = alpha * l_ref[rr, :] + sum(p[:, 128 * t:128 * (t + 1)] for t in range(SEL_CHUNK // 128))
            alpha_ref[rr, :] = alpha
            p_ref[rr, :] = p.astype(BF16)
        acc_ref[...] = alpha_ref[:, 0:HEAD_DIM] * acc_ref[...] + _dot(p_ref[...], vc)

    def body(c, carry):
        @pl.when(((word >> c) & 1) == 1)
        def _():
            chunk_step(c, False)
        return carry

    lax.fori_loop(0, diag_chunk, body, 0)
    chunk_step(diag_chunk, True)
    l_sel = jnp.sum(l_ref[...], axis=1, keepdims=True)
    o_sel = acc_ref[...] * (1.0 / jnp.maximum(l_sel, 1e-30))

    wlen = WINDOW + tq
    ws = pl.multiple_of(jnp.maximum(t0 - WINDOW, 0), tq)
    kwin = kw_ref[0, 0, pl.ds(ws, wlen), :]
    vwin = vw_ref[0, 0, pl.ds(ws, wlen), :]
    dist = ((t0 - ws) + lax.broadcasted_iota(jnp.int32, (tq, wlen), 0)
            - lax.broadcasted_iota(jnp.int32, (tq, wlen), 1))
    wmask = (dist >= 0) & (dist < WINDOW)
    dist_f = dist.astype(F32)
    wbias = jnp.concatenate([jnp.where(wmask, -c * dist_f, MASK_VALUE) for c in coefs], axis=0)
    sw = _dot_nt(q4, kwin) + wbias
    ew = jnp.exp2(sw - jnp.max(sw, axis=1, keepdims=True))
    lw = jnp.sum(ew, axis=1, keepdims=True)
    o_win = _dot(ew.astype(BF16), vwin) * (1.0 / jnp.maximum(lw, 1e-30))

    gt = gate_ref[0, 0]
    outs = []
    for p in range(NSA_HPG):
        r = slice(p * tq, (p + 1) * tq)
        o = (gt[:, p:p + 1] * ocmp_ref[0, p]
             + gt[:, 4 + p:5 + p] * o_sel[r]
             + gt[:, 8 + p:9 + p] * o_win[r])
        outs.append(o)
    o_ref[0] = jnp.concatenate(outs, axis=1).astype(BF16)


def _sel_win(bits, coefs, norms, q, k128, kv, nsel, ocmp, gates, tq=NSA_TQ):
    B, H, S, _ = q.shape
    G = NSA_GROUPS
    nsb = nsel.shape[3]
    nq = S // tq
    rows = NSA_HPG * tq
    kern = functools.partial(_sel_win_kernel, tq=tq)

    def resident(branch, width):
        return pl.BlockSpec((1, 1, 1, S, width), lambda b, g, i, *_: (branch, b, g, 0, 0))

    def kv_kernel(bits_ref, coef_ref, norm_ref, q_ref, ks_ref, kw_ref, vs_ref, vw_ref, *rest):
        kern(bits_ref, coef_ref, norm_ref, q_ref, ks_ref.at[0], kw_ref.at[0], vs_ref.at[0], vw_ref.at[0],
             *rest)

    return pl.pallas_call(
        kv_kernel,
        grid_spec=pltpu.PrefetchScalarGridSpec(
            num_scalar_prefetch=3,
            grid=(B, G, nq),
            in_specs=[pl.BlockSpec((1, NSA_HPG, tq, NSA_ROW), lambda b, g, i, *_: (b, g, i, 0)),
                      resident(0, NSA_ROW), resident(1, NSA_ROW),
                      resident(2, HEAD_DIM), resident(3, HEAD_DIM),
                      pl.BlockSpec((1, 1, tq, nsb), lambda b, g, i, *_: (b, g, i, 0)),
                      pl.BlockSpec((1, NSA_HPG, tq, HEAD_DIM), lambda b, g, i, *_: (b, g, i, 0)),
                      pl.BlockSpec((1, 1, tq, NSA_GATE_STRIDE), lambda b, g, i, *_: (b, g, i, 0))],
            out_specs=pl.BlockSpec((1, tq, NSA_HPG * HEAD_DIM), lambda b, g, i, *_: (b, i, g)),
            scratch_shapes=[pltpu.VMEM((rows, SEL_CHUNK), F32), pltpu.VMEM((rows, SEL_CHUNK), BF16),
                            pltpu.VMEM((rows, 128), F32), pltpu.VMEM((rows, 128), F32),
                            pltpu.VMEM((rows, 128), F32), pltpu.VMEM((rows, HEAD_DIM), F32)]),
        out_shape=jax.ShapeDtypeStruct((B, S, H * HEAD_DIM), BF16),
        compiler_params=_cparams(3),
        name="nsa_sel_win",
    )(bits, coefs, norms, q, k128, k128, kv, kv, nsel, ocmp, gates)


def _out_proj_kernel(o_ref, w_ref, x_ref, y_ref):
    y_ref[0] = x_ref[0] + _dot(o_ref[0], w_ref[...])


def _out_proj(o, w, x, tm=512):
    B, S, D = x.shape
    return pl.pallas_call(
        _out_proj_kernel,
        grid=(B, S // tm),
        in_specs=[pl.BlockSpec((1, tm, o.shape[2]), lambda b, i: (b, i, 0)),
                  pl.BlockSpec(w.shape, lambda b, i: (0, 0)),
                  pl.BlockSpec((1, tm, D), lambda b, i: (b, i, 0))],
        out_specs=pl.BlockSpec((1, tm, D), lambda b, i: (b, i, 0)),
        out_shape=jax.ShapeDtypeStruct((B, S, D), F32),
        compiler_params=_cparams(2),
        name="out_proj",
    )(o, w, x)


def _max_row_norm_sq(x):
    xf = x.astype(F32)
    return jnp.max(jnp.sum(xf * xf, axis=1, keepdims=True), axis=0, keepdims=True)


def _diff_proj_kernel(x_ref, g_ref, w_ref, eq_ref, ek_ref, q_ref, k_ref, v_ref, n_ref):
    h = _rms(x_ref[0], g_ref[...]).astype(BF16)
    nqk = DIFF_HEADS * DIFF_V
    lane = lax.broadcasted_iota(jnp.int32, (8, 128), 1)
    norms = jnp.zeros((8, 128), F32)
    for hd in range(DIFF_HEADS):
        lo, mid, hi = DIFF_QK * hd, DIFF_QK * hd + DIFF_V, DIFF_QK * (hd + 1)
        qh = _dot(h, w_ref[:, DIFF_V * hd:DIFF_V * (hd + 1)]).astype(BF16)
        kh = _dot(h, w_ref[:, nqk + DIFF_V * hd:nqk + DIFF_V * (hd + 1)]).astype(BF16)
        q_ref[0, :, lo:mid] = qh
        q_ref[0, :, mid:hi] = eq_ref[:, DIFF_V * hd:DIFF_V * (hd + 1)]
        k_ref[0, :, lo:mid] = kh
        k_ref[0, :, mid:hi] = ek_ref[:, DIFF_V * hd:DIFF_V * (hd + 1)]
        norms = jnp.where(lane == hd, _max_row_norm_sq(qh), norms)
        norms = jnp.where(lane == DIFF_HEADS + hd, _max_row_norm_sq(kh), norms)
    n_ref[0, 0] = norms
    for c in range(nqk // 512):
        v_ref[0, :, 512 * c:512 * (c + 1)] = _dot(h, w_ref[:, 2 * nqk + 512 * c:2 * nqk + 512 * (c + 1)]).astype(BF16)


def _diff_proj(x, g, w, eq, ek):
    B, S, D = x.shape
    tm = POS_PERIOD
    nqk = DIFF_HEADS * DIFF_QK
    nv = DIFF_HEADS * DIFF_V
    row = lambda b, i: (b, i, 0)
    const = lambda b, i: (0, 0)
    return pl.pallas_call(
        _diff_proj_kernel,
        grid=(B, S // tm),
        in_specs=[pl.BlockSpec((1, tm, D), row),
                  pl.BlockSpec((1, D), const),
                  pl.BlockSpec(w.shape, const),
                  pl.BlockSpec(eq.shape, const),
                  pl.BlockSpec(ek.shape, const)],
        out_specs=[pl.BlockSpec((1, tm, nqk), row), pl.BlockSpec((1, tm, nqk), row),
                   pl.BlockSpec((1, tm, nv), row),
                   pl.BlockSpec((1, 1, 8, 128), lambda b, i: (b, i, 0, 0))],
        out_shape=[jax.ShapeDtypeStruct((B, S, nqk), BF16), jax.ShapeDtypeStruct((B, S, nqk), BF16),
                   jax.ShapeDtypeStruct((B, S, nv), BF16),
                   jax.ShapeDtypeStruct((B, S // tm, 8, 128), F32)],
        compiler_params=_cparams(2),
        name="diff_proj",
    )(x, g, w, eq, ek)


def _diff_attn_kernel(coef_ref, norm_ref, q_ref, k_ref, v_ref, lam_ref, sg_ref, o_ref,
                      *scratch, tq, lambda_init):
    s_refs = scratch[0:DIFF_GROUP]
    p_refs = scratch[DIFF_GROUP:2 * DIFF_GROUP]
    al_refs = scratch[2 * DIFF_GROUP:3 * DIFF_GROUP]
    m_ref, l_ref, acc_ref = scratch[3 * DIFF_GROUP:]
    h = pl.program_id(1)
    i = pl.program_id(2)
    coef = coef_ref[h]
    q = q_ref[0]
    lane = lax.broadcasted_iota(jnp.int32, q.shape, 1)
    zero = jnp.zeros_like(q)
    qs = (jnp.where((lane < HEAD_DIM) | (lane >= DIFF_V), q, zero), jnp.where(lane >= HEAD_DIM, q, zero))

    m_ref[...] = jnp.full(m_ref.shape, MASK_VALUE, F32)
    l_ref[...] = jnp.zeros(l_ref.shape, F32)
    acc_ref[...] = jnp.zeros(acc_ref.shape, F32)

    def qk(j, s_ref):
        kt = k_ref[0, pl.ds(pl.multiple_of(j * tq, tq), tq), :]
        for c in range(2):
            s_ref[c] = _dot_nt(qs[c], kt)

    def softmax(j, s_ref, p_ref, alpha_ref, diag):
        base = coef * ((i - j) * tq).astype(F32)
        for c in range(2):
            for r in range(tq // ROW_BLOCK):
                rows = slice(r * ROW_BLOCK, (r + 1) * ROW_BLOCK)
                s = s_ref[c, rows, :]
                if diag:
                    col = lax.broadcasted_iota(jnp.int32, s.shape, 1)
                    row = r * ROW_BLOCK + lax.broadcasted_iota(jnp.int32, s.shape, 0)
                    s = jnp.where(col <= row, s, MASK_VALUE)
                m_old = m_ref[c, rows, :]
                m_new = jnp.maximum(m_old, jnp.max(s, axis=1, keepdims=True) - base)
                p = jnp.exp2(s - _lane_repeat(m_new + base, tq // 128))
                alpha = jnp.exp2(m_old - m_new)
                m_ref[c, rows, :] = m_new
                l_ref[c, rows, :] = (alpha * l_ref[c, rows, :]
                                     + sum(p[:, 128 * t:128 * (t + 1)] for t in range(tq // 128)))
                alpha_ref[c, rows, :] = alpha
                p_ref[c, rows, :] = p.astype(BF16)

    def pv(j, p_ref, alpha_ref):
        vt = v_ref[0, pl.ds(pl.multiple_of(j * tq, tq), tq), :]
        for c in range(2):
            acc_ref[c] = alpha_ref[c] * acc_ref[c] + _dot(p_ref[c], vt)

    def tile(j, r, diag):
        qk(j, s_refs[r])
        softmax(j, s_refs[r], p_refs[r], al_refs[r], diag)
        pv(j, p_refs[r], al_refs[r])

    tile(i, 0, True)

    b = pl.program_id(0)
    n_tiles = pl.num_programs(2)
    row0 = (b * DIFF_HEADS + h) * n_tiles
    k_off = pl.num_programs(0) * DIFF_HEADS * n_tiles
    qn = norm_ref[row0 + i]
    kn_diag = norm_ref[k_off + row0 + i]

    def leading_dead(j, first):
        gap = ((i - j - 1) * tq + 1).astype(F32)
        bound = qn * (norm_ref[k_off + row0 + j] + kn_diag) * NORM_SLACK + 1.0 - coef * gap
        return jnp.where((first == j) & (bound < -UNDERFLOW_LOG2), j + 1, first)

    first = lax.fori_loop(0, i, leading_dead, 0)
    group = len(s_refs)
    n_groups = (i - first) // group

    def body(jj, carry):
        for r in range(group):
            tile(first + group * jj + r, r, False)
        return carry

    lax.fori_loop(0, n_groups, body, 0)

    def rest(j, carry):
        tile(j, 0, False)
        return carry

    lax.fori_loop(first + n_groups * group, i, rest, 0)

    lv = lam_ref[...]
    lam = (jnp.exp(jnp.sum(lv[0:1] * lv[1:2], axis=1, keepdims=True))
           - jnp.exp(jnp.sum(lv[2:3] * lv[3:4], axis=1, keepdims=True)) + lambda_init)
    l0 = jnp.sum(l_ref[0], axis=1, keepdims=True)
    l1 = jnp.sum(l_ref[1], axis=1, keepdims=True)
    o = (acc_ref[0] * (1.0 / jnp.maximum(l0, 1e-30))
         - lam * (acc_ref[1] * (1.0 / jnp.maximum(l1, 1e-30))))
    o = _rms(o, sg_ref[...]) * (1.0 - lambda_init)
    o_ref[0] = o.astype(BF16)


def _diff_attn(coefs, norms, qx, kx, v, lam_vecs, subln_g, lambda_init):
    B, S, _ = v.shape
    H = DIFF_HEADS
    tq = POS_PERIOD
    kern = functools.partial(_diff_attn_kernel, tq=tq, lambda_init=lambda_init)
    return pl.pallas_call(
        kern,
        grid_spec=pltpu.PrefetchScalarGridSpec(
            num_scalar_prefetch=2,
            grid=(B, H, S // tq),
            in_specs=[pl.BlockSpec((1, tq, DIFF_QK), lambda b, h, i, *_: (b, i, h)),
                      pl.BlockSpec((1, S, DIFF_QK), lambda b, h, i, *_: (b, 0, h)),
                      pl.BlockSpec((1, S, DIFF_V), lambda b, h, i, *_: (b, 0, h)),
                      pl.BlockSpec((4, HEAD_DIM), lambda b, h, i, *_: (0, 0)),
                      pl.BlockSpec((1, DIFF_V), lambda b, h, i, *_: (0, 0))],
            out_specs=pl.BlockSpec((1, tq, DIFF_V), lambda b, h, i, *_: (b, i, h)),
            scratch_shapes=([pltpu.VMEM((2, tq, tq), F32)] * DIFF_GROUP
                            + [pltpu.VMEM((2, tq, tq), BF16)] * DIFF_GROUP
                            + [pltpu.VMEM((2, tq, 128), F32)] * DIFF_GROUP
                            + [pltpu.VMEM((2, tq, 128), F32), pltpu.VMEM((2, tq, 128), F32),
                               pltpu.VMEM((2, tq, DIFF_V), F32)])),
        out_shape=jax.ShapeDtypeStruct((B, S, H * DIFF_V), BF16),
        compiler_params=_cparams(3),
        name="diff_attn",
    )(coefs, norms, qx, kx, v, lam_vecs, subln_g)


def _ffn_kernel(x_ref, g_ref, wup_ref, cw_ref, cb_ref, wdn_ref, fg_ref, y_ref, ug_ref, uv_ref,
                *, tm, final_norm):
    i = pl.program_id(1)

    @pl.when(i == 0)
    def _():
        ug_ref[0:8, :] = jnp.zeros((8, D_FF), F32)
        uv_ref[0:8, :] = jnp.zeros((8, D_FF), F32)

    x = x_ref[0]
    h = _rms(x, g_ref[...]).astype(BF16)
    acc = jnp.zeros((tm, D_MODEL), F32)
    for c in range(D_FF // FF_CHUNK):
        halves = []
        for u_ref, col0 in ((ug_ref, FF_CHUNK * c), (uv_ref, D_FF + FF_CHUNK * c)):
            cols = slice(FF_CHUNK * c, FF_CHUNK * (c + 1))
            u_ref[8:8 + tm, cols] = _dot(h, wup_ref[:, col0:col0 + FF_CHUNK])
            w = cw_ref[:, col0:col0 + FF_CHUNK]
            y = (u_ref[6:6 + tm, cols] * w[0:1] + u_ref[7:7 + tm, cols] * w[1:2]
                 + u_ref[8:8 + tm, cols] * w[2:3] + cb_ref[:, col0:col0 + FF_CHUNK])
            u_ref[0:8, cols] = u_ref[tm:tm + 8, cols]
            halves.append(y)
        act = (jax.nn.silu(halves[0]) * halves[1]).astype(BF16)
        acc = acc + _dot(act, wdn_ref[FF_CHUNK * c:FF_CHUNK * (c + 1), :])
    out = x + acc
    if final_norm:
        out = _rms(out, fg_ref[...])
    y_ref[0] = out


def _ffn(x, g, w_up, conv_w, conv_b, w_down, final_g, final_norm, tm=512):
    B, S, D = x.shape
    kern = functools.partial(_ffn_kernel, tm=tm, final_norm=final_norm)
    const = lambda b, i: (0, 0)
    return pl.pallas_call(
        kern,
        grid=(B, S // tm),
        in_specs=[pl.BlockSpec((1, tm, D), lambda b, i: (b, i, 0)),
                  pl.BlockSpec((1, D), const),
                  pl.BlockSpec(w_up.shape, const, pipeline_mode=pl.Buffered(1)),
                  pl.BlockSpec(conv_w.shape, const),
                  pl.BlockSpec(conv_b.shape, const),
                  pl.BlockSpec(w_down.shape, const, pipeline_mode=pl.Buffered(1)),
                  pl.BlockSpec((1, D), const)],
        out_specs=pl.BlockSpec((1, tm, D), lambda b, i: (b, i, 0)),
        out_shape=jax.ShapeDtypeStruct((B, S, D), F32),
        scratch_shapes=[pltpu.VMEM((tm + 8, D_FF), F32), pltpu.VMEM((tm + 8, D_FF), F32)],
        compiler_params=_cparams(2),
        name="conv_ffn",
    )(x, g, w_up, conv_w, conv_b, w_down, final_g)


def _alibi_slopes(n):
    return jnp.exp2(-8.0 * jnp.arange(1, n + 1, dtype=F32) / n)


def _nsa_weight(w_in):
    gates = w_in[:, NSA_QKV_COLS:].reshape(D_MODEL, NSA_GROUPS, NSA_HPG, 3)
    gates = gates.transpose(0, 1, 3, 2).reshape(D_MODEL, NSA_GROUPS, 3 * NSA_HPG)
    gates = jnp.pad(gates, ((0, 0), (0, 0), (0, NSA_GATE_STRIDE - 3 * NSA_HPG)))
    gates = gates.reshape(D_MODEL, NSA_GROUPS * NSA_GATE_STRIDE)
    gates = jnp.pad(gates, ((0, 0), (0, NSA_GATE_PAD - NSA_GROUPS * NSA_GATE_STRIDE)))

    def widen(cols):
        w = w_in[:, cols].reshape(D_MODEL, -1, HEAD_DIM)
        return jnp.pad(w, ((0, 0), (0, 0), (0, NSA_ROW - HEAD_DIM))).reshape(D_MODEL, -1)

    nq, nk = NSA_HEADS * HEAD_DIM, NSA_GROUPS * HEAD_DIM
    kc, vc, ks, vs, kw, vw = [slice(nq + nk * b, nq + nk * (b + 1)) for b in range(6)]
    return jnp.concatenate([widen(slice(0, nq)), widen(ks), widen(kw),
                            w_in[:, kc], w_in[:, vc], w_in[:, vs], w_in[:, vw], gates], axis=1).astype(BF16)


def _coef_pieces(coefs):
    c_hi = coefs.astype(BF16)
    c_mid = (coefs - c_hi.astype(F32)).astype(BF16)
    c_lo = (coefs - c_hi.astype(F32) - c_mid.astype(F32)).astype(BF16)
    return jnp.stack([c_hi, c_hi, c_mid, c_mid, c_lo, c_lo], axis=1)


def _nsa_columns(coefs, nc):
    def place(cols, lane0):
        return jnp.pad(cols.astype(F32), ((0, 0), (lane0, NSA_ROW - lane0 - cols.shape[1])))

    q_cols = place(_coef_pieces(coefs), COL_COEF)
    pos = jnp.arange(SEL_CHUNK)
    k_pos = jnp.stack([pos - pos % 2, pos % 2] * 3, axis=1)
    k_mask = jnp.where(pos[:, None] // SEL_BLOCK == jnp.arange(BLOCKS_PER_CHUNK)[None, :], MASK_BIG, 0.0)
    k_cols = place(k_pos, COL_COEF) + place(k_mask, COL_MASK)
    j = jnp.arange(nc)
    c_pos = jnp.stack([4 * CMP_STRIDE * (j // 4), CMP_STRIDE * (j % 4)] * 3, axis=1)
    cmp_cols = jnp.stack([place(c_pos, COL_COEF), jnp.zeros((nc, NSA_ROW), F32)])
    return q_cols, k_cols, cmp_cols


def _nsa_layer(x, norm_g, w_in, k_pe, k_w1, k_w2, v_pe, v_w1, v_w2, w_out):
    B, S, _ = x.shape
    nc = S // CMP_STRIDE
    coefs = _alibi_slopes(NSA_HEADS) * LOG2E
    q_cols, k_cols, cmp_cols = _nsa_columns(coefs, nc)
    q, k128, kv, gates, nsq = _nsa_proj(x, norm_g[None], _nsa_weight(w_in), q_cols, k_cols)
    n_q = (SEL_CHUNK // NSA_TQ) * NSA_HEADS
    norms = jnp.sqrt(jnp.concatenate([nsq[:, :, 0, :n_q].reshape(-1),
                                      nsq[:, :, 0, n_q:n_q + NSA_GROUPS].reshape(-1)]))

    xc = kv[0:2].reshape(2, B, NSA_GROUPS, nc, CMP_STRIDE * HEAD_DIM)
    pe8 = jnp.broadcast_to(jnp.stack([k_pe, v_pe]).reshape(2, 1, CMP_BLOCK * HEAD_DIM),
                           (2, 8, CMP_BLOCK * HEAD_DIM)).astype(BF16)
    w2 = jnp.pad(jnp.stack([k_w2, v_w2]), ((0, 0), (0, 0), (0, NSA_ROW - HEAD_DIM))).astype(BF16)
    cmp = _compress(xc, pe8, jnp.stack([k_w1, v_w1]).astype(BF16), w2, cmp_cols)
    kcmp, vcmp_t = cmp[0], cmp[1][..., :HEAD_DIM].transpose(0, 1, 3, 2)

    nsb = S // SEL_BLOCK
    ratio = SEL_BLOCK // CMP_STRIDE
    sb = jnp.arange(nsb)[:, None]
    cb = jnp.arange(nc)[None, :]
    pool_t = ((cb >= ratio * sb - 1) & (cb <= ratio * sb + ratio - 1)).astype(BF16)
    grp = (jnp.arange(nsb)[:, None] // BLOCKS_PER_CHUNK == jnp.arange(128)[None, :]).astype(BF16)
    ocmp, nsel, bits = _cmp_topk(q, kcmp, vcmp_t, pool_t, grp)
    word = bits[:, 0, 0] | (bits[:, 0, 1] << 16)
    o = _sel_win(word, coefs, norms, q, k128, kv, nsel, ocmp, gates)
    return _out_proj(o, w_out.astype(BF16), x)


def _alibi_columns(coefs, period):
    pieces = _coef_pieces(coefs)
    pos = jnp.arange(period)
    parts = jnp.stack([pos - pos % 2, pos % 2] * 3, axis=1).astype(BF16)
    n = coefs.shape[0]
    pieces_b = jnp.broadcast_to(pieces[None], (period, n, 6))
    parts_b = jnp.broadcast_to(parts[:, None], (period, n, 6))
    pad = jnp.zeros((period, n, DIFF_V - 12), BF16)
    eq = jnp.concatenate([pieces_b, parts_b, pad], axis=2).reshape(period, n * DIFF_V)
    ek = jnp.concatenate([parts_b, -pieces_b, pad], axis=2).reshape(period, n * DIFF_V)
    return eq, ek


def _diff_layer(x, norm_g, w_in, lq1, lk1, lq2, lk2, subln_g, w_out, lambda_init):
    nqk = DIFF_HEADS * DIFF_V
    scale = jnp.concatenate([jnp.full((nqk,), HEAD_DIM ** -0.5 * LOG2E, F32), jnp.ones((2 * nqk,), F32)])
    coefs = _alibi_slopes(DIFF_HEADS) * LOG2E
    eq, ek = _alibi_columns(coefs, POS_PERIOD)
    qx, kx, v, nsq = _diff_proj(x, norm_g[None], (w_in * scale[None, :]).astype(BF16), eq, ek)
    nsq = nsq[:, :, 0, :2 * DIFF_HEADS].reshape(nsq.shape[0], nsq.shape[1], 2, DIFF_HEADS)
    norms = jnp.sqrt(nsq).transpose(2, 0, 3, 1).reshape(-1)
    lam_vecs = jnp.stack([lq1, lk1, lq2, lk2]).astype(F32)
    o = _diff_attn(coefs, norms, qx, kx, v, lam_vecs, subln_g[None].astype(F32), lambda_init)
    return _out_proj(o, w_out.astype(BF16), x)


def kernel(x, norm_mix_g, norm_ffn_g, final_norm_g, nsa_w_in, nsa_cmp_k_pe, nsa_cmp_k_w1, nsa_cmp_k_w2, nsa_cmp_v_pe, nsa_cmp_v_w1, nsa_cmp_v_w2, nsa_w_out, diff_w_in, diff_lam_q1, diff_lam_k1, diff_lam_q2, diff_lam_k2, diff_subln_g, diff_w_out, ffn_w_up, ffn_conv_w, ffn_conv_b, ffn_w_down):
    depth = norm_mix_g.shape[0]
    for i in range(depth):
        j = i // 2
        if i % 2 == 0:
            x = _nsa_layer(x, norm_mix_g[i], nsa_w_in[j], nsa_cmp_k_pe[j], nsa_cmp_k_w1[j], nsa_cmp_k_w2[j],
                           nsa_cmp_v_pe[j], nsa_cmp_v_w1[j], nsa_cmp_v_w2[j], nsa_w_out[j])
        else:
            lambda_init = 0.8 - 0.6 * math.exp(-0.3 * i)
            x = _diff_layer(x, norm_mix_g[i], diff_w_in[j], diff_lam_q1[j], diff_lam_k1[j], diff_lam_q2[j],
                            diff_lam_k2[j], diff_subln_g[j], diff_w_out[j], lambda_init)
        x = _ffn(x, norm_ffn_g[i][None], ffn_w_up[i].astype(BF16), ffn_conv_w[i], ffn_conv_b[i][None],
                 ffn_w_down[i].astype(BF16), final_norm_g[None], final_norm=(i == depth - 1))
    return x
```

```python
import functools
import math

import jax
import jax.numpy as jnp
from jax import lax
from jax.experimental import pallas as pl
from jax.experimental.pallas import tpu as pltpu

F32 = jnp.float32
BF16 = jnp.bfloat16

D_MODEL = 1024
EPS = 1e-6
MASK_VALUE = -1e30
FORCED_SCORE = 1e6

NSA_HEADS = 16
NSA_GROUPS = 4
NSA_HPG = 4
HEAD_DIM = 64
CMP_STRIDE = 16
CMP_BLOCK = 32
CMP_HIDDEN = 256
SEL_BLOCK = 64
SEL_TOP_N = 16
WINDOW = 512
NSA_QKV_COLS = NSA_HEADS * HEAD_DIM + 6 * NSA_GROUPS * HEAD_DIM
NSA_GATE_PAD = 128
NSA_GATE_STRIDE = 16

DIFF_HEADS = 8
DIFF_V = 128
DIFF_QK = 256
POS_PERIOD = 512
ROW_BLOCK = 32
DIFF_GROUP = 3
LOG2E = 1.4426950408889634
UNDERFLOW_LOG2 = 160.0
NORM_SLACK = 1.01

D_FF = 2816
FF_CHUNK = 256

SEL_CHUNK = 512
BLOCKS_PER_CHUNK = SEL_CHUNK // SEL_BLOCK
MASK_BIG = -(2.0 ** 100)
NO_KEY_FLOOR = -1e29
NSA_ROW = 128
COL_COEF = 64
COL_MASK = 70
CMP_CHUNK = 256
NSA_TQ = 256
CMP_TQ = 512

VMEM_LIMIT = 56 * 1024 * 1024


def _cparams(n_axes):
    return pltpu.CompilerParams(dimension_semantics=("arbitrary",) * n_axes,
                                vmem_limit_bytes=VMEM_LIMIT)


def _rms(x, g):
    return x * lax.rsqrt(jnp.mean(x * x, axis=-1, keepdims=True) + EPS) * g


def _dot(a, b):
    return jnp.dot(a, b, preferred_element_type=F32)


def _lane_repeat(x, n):
    return jnp.concatenate([x] * n, axis=1)


def _dot_nt(a, b):
    return lax.dot_general(a, b, (((1,), (1,)), ((), ())), preferred_element_type=F32)


def _nsa_proj_kernel(x_ref, g_ref, w_ref, qc_ref, kc_ref, q_ref, k_ref, kv_ref, gate_ref, n_ref):
    h = _rms(x_ref[0], g_ref[...]).astype(BF16)
    tm = h.shape[0]
    lane = lax.broadcasted_iota(jnp.int32, (8, 128), 1)
    norms = jnp.zeros((8, 128), F32)
    col = 0
    for pair in range(NSA_HEADS // 2):
        pc = _dot(h, w_ref[:, col:col + 2 * NSA_ROW]) * (HEAD_DIM ** -0.5 * LOG2E)
        for s in range(2):
            hd = 2 * pair + s
            qh = pc[:, NSA_ROW * s:NSA_ROW * (s + 1)].astype(BF16)
            q_ref[0, hd] = qh + qc_ref[hd:hd + 1, :].astype(BF16)
            for sub in range(tm // NSA_TQ):
                part = _max_row_norm_sq(qh[sub * NSA_TQ:(sub + 1) * NSA_TQ])
                norms = jnp.where(lane == sub * NSA_HEADS + hd, part, norms)
        col += 2 * NSA_ROW
    for br in range(2):
        for pair in range(NSA_GROUPS // 2):
            pc = _dot(h, w_ref[:, col:col + 2 * NSA_ROW])
            for s in range(2):
                piece = pc[:, NSA_ROW * s:NSA_ROW * (s + 1)].astype(BF16)
                if br == 0:
                    k_ref[br, 0, 2 * pair + s] = piece + kc_ref[...].astype(BF16)
                    norms = jnp.where(lane == (tm // NSA_TQ) * NSA_HEADS + 2 * pair + s,
                                      _max_row_norm_sq(piece), norms)
                else:
                    k_ref[br, 0, 2 * pair + s] = piece
            col += 2 * NSA_ROW
    n_ref[0, 0] = norms
    for br in range(4):
        pc = _dot(h, w_ref[:, col:col + 256])
        for g in range(NSA_GROUPS):
            kv_ref[br, 0, g] = pc[:, 64 * g:64 * (g + 1)].astype(BF16)
        col += 256
    pg = _dot(h, w_ref[:, col:col + NSA_GATE_PAD])
    for g in range(NSA_GROUPS):
        gate_ref[0, g] = jax.nn.sigmoid(pg[:, NSA_GATE_STRIDE * g:NSA_GATE_STRIDE * (g + 1)])


def _nsa_proj(x, g, w, q_cols, k_cols):
    B, S, D = x.shape
    tm = SEL_CHUNK
    n = w.shape[1]
    return pl.pallas_call(
        _nsa_proj_kernel,
        grid=(B, S // tm),
        in_specs=[pl.BlockSpec((1, tm, D), lambda b, i: (b, i, 0)),
                  pl.BlockSpec((1, D), lambda b, i: (0, 0)),
                  pl.BlockSpec((D, n), lambda b, i: (0, 0)),
                  pl.BlockSpec(q_cols.shape, lambda b, i: (0, 0)),
                  pl.BlockSpec(k_cols.shape, lambda b, i: (0, 0))],
        out_specs=[pl.BlockSpec((1, NSA_HEADS, tm, NSA_ROW), lambda b, i: (b, 0, i, 0)),
                   pl.BlockSpec((2, 1, NSA_GROUPS, tm, NSA_ROW), lambda b, i: (0, b, 0, i, 0)),
                   pl.BlockSpec((4, 1, NSA_GROUPS, tm, HEAD_DIM), lambda b, i: (0, b, 0, i, 0)),
                   pl.BlockSpec((1, NSA_GROUPS, tm, NSA_GATE_STRIDE), lambda b, i: (b, 0, i, 0)),
                   pl.BlockSpec((1, 1, 8, 128), lambda b, i: (b, i, 0, 0))],
        out_shape=[jax.ShapeDtypeStruct((B, NSA_HEADS, S, NSA_ROW), BF16),
                   jax.ShapeDtypeStruct((2, B, NSA_GROUPS, S, NSA_ROW), BF16),
                   jax.ShapeDtypeStruct((4, B, NSA_GROUPS, S, HEAD_DIM), BF16),
                   jax.ShapeDtypeStruct((B, NSA_GROUPS, S, NSA_GATE_STRIDE), F32),
                   jax.ShapeDtypeStruct((B, S // tm, 8, 128), F32)],
        compiler_params=_cparams(2),
        name="nsa_proj",
    )(x, g, w, q_cols, k_cols)


def _compress_kernel(x_ref, pe_ref, w1_ref, w2_ref, cols_ref, o_ref, shift_ref):
    nc = x_ref.shape[3]
    half = CMP_STRIDE * HEAD_DIM
    x = x_ref[0, 0, 0]
    first = _dot(x, w1_ref[0, :half, :])
    second = _dot(x, w1_ref[0, half:, :])
    shift_ref[0:nc, :] = second
    shift_ref[nc:nc + 8, :] = jnp.zeros((8, CMP_HIDDEN), F32)
    pe_term = _dot(pe_ref[0], w1_ref[0])[0:1, :]
    pre = first + shift_ref[pl.ds(1, nc), :] + pe_term
    hid = jax.nn.gelu(pre, approximate=True)
    o_ref[0, 0, 0] = (_dot(hid.astype(BF16), w2_ref[0]) + cols_ref[0]).astype(BF16)


def _compress(xc, pe8, w1, w2, cols):
    _, B, G, nc, width = xc.shape
    return pl.pallas_call(
        _compress_kernel,
        grid=(2, B, G),
        in_specs=[pl.BlockSpec((1, 1, 1, nc, width), lambda a, b, g: (a, b, g, 0, 0)),
                  pl.BlockSpec((1, 8, CMP_BLOCK * HEAD_DIM), lambda a, b, g: (a, 0, 0)),
                  pl.BlockSpec((1, CMP_BLOCK * HEAD_DIM, CMP_HIDDEN), lambda a, b, g: (a, 0, 0)),
                  pl.BlockSpec((1, CMP_HIDDEN, NSA_ROW), lambda a, b, g: (a, 0, 0)),
                  pl.BlockSpec((1, nc, NSA_ROW), lambda a, b, g: (a, 0, 0))],
        out_specs=pl.BlockSpec((1, 1, 1, nc, NSA_ROW), lambda a, b, g: (a, b, g, 0, 0)),
        out_shape=jax.ShapeDtypeStruct((2, B, G, nc, NSA_ROW), BF16),
        scratch_shapes=[pltpu.VMEM((nc + 8, CMP_HIDDEN), F32)],
        compiler_params=_cparams(3),
        name="nsa_compress",
    )(xc, pe8, w1, w2, cols)


def _cmp_topk_kernel(q_ref, kc_ref, vct_ref, pool_ref, grp_ref, ocmp_ref, nsel_ref, bits_ref, *, tq):
    i = pl.program_id(2)
    ncp = kc_ref.shape[2]
    nsel_blocks = pool_ref.shape[0]
    t0 = i * tq
    ratio = SEL_BLOCK // CMP_STRIDE

    def visible_prefix(k, three_forced):
        rows = CMP_CHUNK * k
        nsb = rows // ratio
        kc = kc_ref[0, 0, 0:rows, :]
        vct = vct_ref[0, 0, :, 0:rows]
        tok = t0 + lax.broadcasted_iota(jnp.int32, (rows, tq), 1)
        cmp_end = lax.broadcasted_iota(jnp.int32, (rows, tq), 0) * CMP_STRIDE + (CMP_BLOCK - 1)
        mask = tok >= cmp_end

        p_grp = jnp.zeros((rows, tq), F32)
        for p in range(NSA_HPG):
            s = jnp.where(mask, _dot_nt(kc, q_ref[0, p]), MASK_VALUE)
            m = jnp.maximum(jnp.max(s, axis=0, keepdims=True), NO_KEY_FLOOR)
            e = jnp.exp2(s - m)
            r = 1.0 / jnp.maximum(jnp.sum(e, axis=0, keepdims=True), 1e-30)
            p_grp = p_grp + e * r
            ocmp_ref[0, p] = (_dot(vct, e.astype(BF16)) * r).T

        hi = p_grp.astype(BF16)
        r1 = p_grp - hi.astype(F32)
        mid = r1.astype(BF16)
        lo = (r1 - mid.astype(F32)).astype(BF16)
        pool = pool_ref[0:nsb, 0:rows]
        imp = _dot(pool, hi) + _dot(pool, mid) + _dot(pool, lo)

        blk = lax.broadcasted_iota(jnp.int32, (nsb, tq), 0)
        cur = (t0 + lax.broadcasted_iota(jnp.int32, (nsb, tq), 1)) // SEL_BLOCK
        valid = blk <= cur
        forced = (blk == 0) | (valid & (blk >= cur - 1))
        blk_f = blk.astype(F32)
        if three_forced:
            work = jnp.where(forced, -jnp.inf, jnp.where(valid, imp, -1.0))
            sel = jnp.where(forced, 1.0, 0.0)
            rounds = min(SEL_TOP_N, nsb) - 3
        else:
            work = jnp.where(forced, FORCED_SCORE, jnp.where(valid, imp, -1.0))
            sel = jnp.zeros((nsb, tq), F32)
            rounds = min(SEL_TOP_N, nsb)
        for _ in range(rounds):
            mx = jnp.max(work, axis=0, keepdims=True)
            first = jnp.min(jnp.where(work == mx, blk_f, float(nsb)), axis=0, keepdims=True)
            pick = blk_f == first
            sel = jnp.where(pick, 1.0, sel)
            work = jnp.where(pick, -jnp.inf, work)
        sel = jnp.where(valid, sel, 0.0)
        if nsb < nsel_blocks:
            sel = jnp.concatenate([sel, jnp.zeros((nsel_blocks - nsb, tq), F32)], axis=0)

        sel_t = sel.T
        nsel_ref[0, 0] = (1.0 - sel_t).astype(BF16)

        lane = lax.broadcasted_iota(jnp.int32, (1, 128), 1)
        bit = jnp.left_shift(1, lane & 15).astype(F32)
        words = jnp.zeros((1, 128), F32)
        for sub in range(tq // NSA_TQ):
            any_blk = jnp.max(sel_t[sub * NSA_TQ:(sub + 1) * NSA_TQ], axis=0, keepdims=True)
            cnt = _dot(jnp.broadcast_to(any_blk, (8, nsel_blocks)).astype(BF16), grp_ref[...])[0:1, :]
            flag = cnt > 0.0
            lo_word = jnp.sum(jnp.where(flag & (lane < 16), bit, 0.0), axis=1, keepdims=True)
            hi_word = jnp.sum(jnp.where(flag & (lane >= 16) & (lane < 32), bit, 0.0), axis=1, keepdims=True)
            words = jnp.where(lane == 2 * sub, lo_word, jnp.where(lane == 2 * sub + 1, hi_word, words))
        bits_ref[0] = words.astype(jnp.int32)

    n_chunks = ncp // CMP_CHUNK
    last_visible = (t0 + tq - CMP_BLOCK) // CMP_STRIDE
    assert tq >= 2 * SEL_BLOCK
    variant = jnp.where(i == 0, 0, 1 + jnp.clip(last_visible // CMP_CHUNK, 0, n_chunks - 1))
    lax.switch(variant, [functools.partial(visible_prefix, 1, False)]
               + [functools.partial(visible_prefix, k, True) for k in range(1, n_chunks + 1)])


def _cmp_topk(q, kcmp, vcmp_t, pool_t, grp, tq=CMP_TQ):
    B, H, S, _ = q.shape
    G = NSA_GROUPS
    ncp = kcmp.shape[2]
    nsb = pool_t.shape[0]
    nq = S // tq
    kern = functools.partial(_cmp_topk_kernel, tq=tq)
    return pl.pallas_call(
        kern,
        grid=(B, G, nq),
        in_specs=[pl.BlockSpec((1, NSA_HPG, tq, NSA_ROW), lambda b, g, i: (b, g, i, 0)),
                  pl.BlockSpec((1, 1, ncp, NSA_ROW), lambda b, g, i: (b, g, 0, 0)),
                  pl.BlockSpec((1, 1, HEAD_DIM, ncp), lambda b, g, i: (b, g, 0, 0)),
                  pl.BlockSpec((nsb, ncp), lambda b, g, i: (0, 0)),
                  pl.BlockSpec((nsb, 128), lambda b, g, i: (0, 0))],
        out_specs=[pl.BlockSpec((1, NSA_HPG, tq, HEAD_DIM), lambda b, g, i: (b, g, i, 0)),
                   pl.BlockSpec((1, 1, tq, nsb), lambda b, g, i: (b, g, i, 0)),
                   pl.BlockSpec((1, 1, 128), lambda b, g, i: ((b * G + g) * nq + i, 0, 0))],
        out_shape=[jax.ShapeDtypeStruct((B, H, S, HEAD_DIM), F32),
                   jax.ShapeDtypeStruct((B, G, S, nsb), BF16),
                   jax.ShapeDtypeStruct((B * G * nq, 1, 128), jnp.int32)],
        compiler_params=_cparams(3),
        name="nsa_cmp_topk",
    )(q, kcmp, vcmp_t, pool_t, grp)


def _sel_win_kernel(bits_ref, coef_ref, norm_ref, q_ref, ks_ref, kw_ref, vs_ref, vw_ref, nsel_ref,
                    ocmp_ref, gate_ref, o_ref, s_ref, p_ref, alpha_ref, m_ref, l_ref, acc_ref, *, tq):
    b = pl.program_id(0)
    g = pl.program_id(1)
    i = pl.program_id(2)
    nq = pl.num_programs(2)
    nsb = nsel_ref.shape[3]
    rows = NSA_HPG * tq
    t0 = i * tq
    q4 = q_ref[0].reshape(rows, NSA_ROW)
    word = bits_ref[((b * NSA_GROUPS + g) * nq + i)]
    coefs = [coef_ref[NSA_HPG * g + p] for p in range(NSA_HPG)]
    coef_rows = jnp.concatenate([jnp.full((tq, 128), c, F32) for c in coefs], axis=0)
    diag_chunk = t0 // SEL_CHUNK

    n_chunks = nq * tq // SEL_CHUNK
    q_row = (b * nq + i) * NSA_HEADS + NSA_HPG * g
    k_off = pl.num_programs(0) * nq * NSA_HEADS
    qn = functools.reduce(jnp.maximum, [norm_ref[q_row + p] for p in range(NSA_HPG)])
    coef_min = functools.reduce(jnp.minimum, coefs)
    kn_diag = norm_ref[k_off + (b * n_chunks + diag_chunk) * NSA_GROUPS + g]

    def prune(c, w):
        gap = (t0 - (c * SEL_CHUNK + SEL_CHUNK - 1)).astype(F32)
        kn = norm_ref[k_off + (b * n_chunks + c) * NSA_GROUPS + g]
        bound = qn * (kn + kn_diag) * NORM_SLACK + 1.0 - coef_min * gap
        return jnp.where(bound < -UNDERFLOW_LOG2, w & ~jnp.left_shift(1, c), w)

    word = lax.fori_loop(0, diag_chunk, prune, word)

    m_ref[...] = jnp.full(m_ref.shape, MASK_VALUE, F32)
    l_ref[...] = jnp.zeros(l_ref.shape, F32)
    acc_ref[...] = jnp.zeros(acc_ref.shape, F32)
    nsel = nsel_ref[0, 0].astype(F32)
    lane = lax.broadcasted_iota(jnp.int32, (tq, NSA_ROW), 1)
    flag_lanes = (lane >= COL_MASK) & (lane < COL_MASK + BLOCKS_PER_CHUNK)
    diag_chunk = t0 // SEL_CHUNK

    def chunk_step(c, causal):
        k0 = pl.multiple_of(c * SEL_CHUNK, SEL_CHUNK)
        kc = ks_ref[0, 0, pl.ds(k0, SEL_CHUNK), :]
        vc = vs_ref[0, 0, pl.ds(k0, SEL_CHUNK), :]
        shift = (COL_MASK + nsb - BLOCKS_PER_CHUNK * c) % nsb
        flags = jnp.where(flag_lanes, pltpu.roll(nsel, shift, 1)[:, :NSA_ROW], 0.0).astype(BF16)
        s_ref[...] = _dot_nt(q4 + jnp.tile(flags, (NSA_HPG, 1)), kc)
        base = coef_rows * (k0 - t0).astype(F32)
        for r in range(rows // ROW_BLOCK):
            rr = slice(r * ROW_BLOCK, (r + 1) * ROW_BLOCK)
            s = s_ref[rr, :]
            if causal:
                col = lax.broadcasted_iota(jnp.int32, s.shape, 1)
                row = (r * ROW_BLOCK) % tq + lax.broadcasted_iota(jnp.int32, s.shape, 0)
                s = jnp.where(col - row <= t0 - k0, s, MASK_VALUE)
            m_old = m_ref[rr, :]
            m_new = jnp.maximum(m_old, jnp.max(s, axis=1, keepdims=True) + base[rr])
            p = jnp.exp2(s - _lane_repeat(m_new - base[rr], SEL_CHUNK // 128))
            alpha = jnp.exp2(m_old - m_new)
            m_ref[rr, :] = m_new
            l_ref[rr, :] = alpha * l_ref[rr, :] + sum(p[:, 128 * t:128 * (t + 1)] for t in range(SEL_CHUNK // 128))
            alpha_ref[rr, :] = alpha
            p_ref[rr, :] = p.astype(BF16)
        acc_ref[...] = alpha_ref[:, 0:HEAD_DIM] * acc_ref[...] + _dot(p_ref[...], vc)

    def body(c, carry):
        @pl.when(((word >> c) & 1) == 1)
        def _():
            chunk_step(c, False)
        return carry

    lax.fori_loop(0, diag_chunk, body, 0)
    chunk_step(diag_chunk, True)
    l_sel = jnp.sum(l_ref[...], axis=1, keepdims=True)
    o_sel = acc_ref[...] * (1.0 / jnp.maximum(l_sel, 1e-30))

    wlen = WINDOW + tq
    ws = pl.multiple_of(jnp.maximum(t0 - WINDOW, 0), tq)
    kwin = kw_ref[0, 0, pl.ds(ws, wlen), :]
    vwin = vw_ref[0, 0, pl.ds(ws, wlen), :]
    dist = ((t0 - ws) + lax.broadcasted_iota(jnp.int32, (tq, wlen), 0)
            - lax.broadcasted_iota(jnp.int32, (tq, wlen), 1))
    wmask = (dist >= 0) & (dist < WINDOW)
    dist_f = dist.astype(F32)
    wbias = jnp.concatenate([jnp.where(wmask, -c * dist_f, MASK_VALUE) for c in coefs], axis=0)
    sw = _dot_nt(q4, kwin) + wbias
    ew = jnp.exp2(sw - jnp.max(sw, axis=1, keepdims=True))
    lw = jnp.sum(ew, axis=1, keepdims=True)
    o_win = _dot(ew.astype(BF16), vwin) * (1.0 / jnp.maximum(lw, 1e-30))

    gt = gate_ref[0, 0]
    outs = []
    for p in range(NSA_HPG):
        r = slice(p * tq, (p + 1) * tq)
        o = (gt[:, p:p + 1] * ocmp_ref[0, p]
             + gt[:, 4 + p:5 + p] * o_sel[r]
             + gt[:, 8 + p:9 + p] * o_win[r])
        outs.append(o)
    o_ref[0] = jnp.concatenate(outs, axis=1).astype(BF16)


def _sel_win(bits, coefs, norms, q, k128, kv, nsel, ocmp, gates, tq=NSA_TQ):
    B, H, S, _ = q.shape
    G = NSA_GROUPS
    nsb = nsel.shape[3]
    nq = S // tq
    rows = NSA_HPG * tq
    kern = functools.partial(_sel_win_kernel, tq=tq)

    def resident(branch, width):
        return pl.BlockSpec((1, 1, 1, S, width), lambda b, g, i, *_: (branch, b, g, 0, 0),
                            pipeline_mode=pl.Buffered(1))

    def kv_kernel(bits_ref, coef_ref, norm_ref, q_ref, ks_ref, kw_ref, vs_ref, vw_ref, *rest):
        kern(bits_ref, coef_ref, norm_ref, q_ref, ks_ref.at[0], kw_ref.at[0], vs_ref.at[0], vw_ref.at[0],
             *rest)

    return pl.pallas_call(
        kv_kernel,
        grid_spec=pltpu.PrefetchScalarGridSpec(
            num_scalar_prefetch=3,
            grid=(B, G, nq),
            in_specs=[pl.BlockSpec((1, NSA_HPG, tq, NSA_ROW), lambda b, g, i, *_: (b, g, i, 0)),
                      resident(0, NSA_ROW), resident(1, NSA_ROW),
                      resident(2, HEAD_DIM), resident(3, HEAD_DIM),
                      pl.BlockSpec((1, 1, tq, nsb), lambda b, g, i, *_: (b, g, i, 0)),
                      pl.BlockSpec((1, NSA_HPG, tq, HEAD_DIM), lambda b, g, i, *_: (b, g, i, 0)),
                      pl.BlockSpec((1, 1, tq, NSA_GATE_STRIDE), lambda b, g, i, *_: (b, g, i, 0))],
            out_specs=pl.BlockSpec((1, tq, NSA_HPG * HEAD_DIM), lambda b, g, i, *_: (b, i, g)),
            scratch_shapes=[pltpu.VMEM((rows, SEL_CHUNK), F32), pltpu.VMEM((rows, SEL_CHUNK), BF16),
                            pltpu.VMEM((rows, 128), F32), pltpu.VMEM((rows, 128), F32),
                            pltpu.VMEM((rows, 128), F32), pltpu.VMEM((rows, HEAD_DIM), F32)]),
        out_shape=jax.ShapeDtypeStruct((B, S, H * HEAD_DIM), BF16),
        compiler_params=_cparams(3),
        name="nsa_sel_win",
    )(bits, coefs, norms, q, k128, k128, kv, kv, nsel, ocmp, gates)


def _out_proj_kernel(o_ref, w_ref, x_ref, y_ref):
    y_ref[0] = x_ref[0] + _dot(o_ref[0], w_ref[...])


def _out_proj(o, w, x, tm=512):
    B, S, D = x.shape
    return pl.pallas_call(
        _out_proj_kernel,
        grid=(B, S // tm),
        in_specs=[pl.BlockSpec((1, tm, o.shape[2]), lambda b, i: (b, i, 0)),
                  pl.BlockSpec(w.shape, lambda b, i: (0, 0)),
                  pl.BlockSpec((1, tm, D), lambda b, i: (b, i, 0))],
        out_specs=pl.BlockSpec((1, tm, D), lambda b, i: (b, i, 0)),
        out_shape=jax.ShapeDtypeStruct((B, S, D), F32),
        compiler_params=_cparams(2),
        name="out_proj",
    )(o, w, x)


def _max_row_norm_sq(x):
    xf = x.astype(F32)
    return jnp.max(jnp.sum(xf * xf, axis=1, keepdims=True), axis=0, keepdims=True)


def _diff_proj_kernel(x_ref, g_ref, w_ref, eq_ref, ek_ref, q_ref, k_ref, v_ref, n_ref):
    h = _rms(x_ref[0], g_ref[...]).astype(BF16)
    nqk = DIFF_HEADS * DIFF_V
    lane = lax.broadcasted_iota(jnp.int32, (8, 128), 1)
    norms = jnp.zeros((8, 128), F32)
    for hd in range(DIFF_HEADS):
        lo, mid, hi = DIFF_QK * hd, DIFF_QK * hd + DIFF_V, DIFF_QK * (hd + 1)
        qh = _dot(h, w_ref[:, DIFF_V * hd:DIFF_V * (hd + 1)]).astype(BF16)
        kh = _dot(h, w_ref[:, nqk + DIFF_V * hd:nqk + DIFF_V * (hd + 1)]).astype(BF16)
        q_ref[0, :, lo:mid] = qh
        q_ref[0, :, mid:hi] = eq_ref[:, DIFF_V * hd:DIFF_V * (hd + 1)]
        k_ref[0, :, lo:mid] = kh
        k_ref[0, :, mid:hi] = ek_ref[:, DIFF_V * hd:DIFF_V * (hd + 1)]
        norms = jnp.where(lane == hd, _max_row_norm_sq(qh), norms)
        norms = jnp.where(lane == DIFF_HEADS + hd, _max_row_norm_sq(kh), norms)
    n_ref[0, 0] = norms
    for c in range(nqk // 512):
        v_ref[0, :, 512 * c:512 * (c + 1)] = _dot(h, w_ref[:, 2 * nqk + 512 * c:2 * nqk + 512 * (c + 1)]).astype(BF16)


def _diff_proj(x, g, w, eq, ek):
    B, S, D = x.shape
    tm = POS_PERIOD
    nqk = DIFF_HEADS * DIFF_QK
    nv = DIFF_HEADS * DIFF_V
    row = lambda b, i: (b, i, 0)
    const = lambda b, i: (0, 0)
    return pl.pallas_call(
        _diff_proj_kernel,
        grid=(B, S // tm),
        in_specs=[pl.BlockSpec((1, tm, D), row),
                  pl.BlockSpec((1, D), const),
                  pl.BlockSpec(w.shape, const),
                  pl.BlockSpec(eq.shape, const),
                  pl.BlockSpec(ek.shape, const)],
        out_specs=[pl.BlockSpec((1, tm, nqk), row), pl.BlockSpec((1, tm, nqk), row),
                   pl.BlockSpec((1, tm, nv), row),
                   pl.BlockSpec((1, 1, 8, 128), lambda b, i: (b, i, 0, 0))],
        out_shape=[jax.ShapeDtypeStruct((B, S, nqk), BF16), jax.ShapeDtypeStruct((B, S, nqk), BF16),
                   jax.ShapeDtypeStruct((B, S, nv), BF16),
                   jax.ShapeDtypeStruct((B, S // tm, 8, 128), F32)],
        compiler_params=_cparams(2),
        name="diff_proj",
    )(x, g, w, eq, ek)


def _diff_attn_kernel(coef_ref, norm_ref, q_ref, k_ref, v_ref, lam_ref, sg_ref, o_ref,
                      *scratch, tq, lambda_init):
    s_refs = scratch[0:DIFF_GROUP]
    p_refs = scratch[DIFF_GROUP:2 * DIFF_GROUP]
    al_refs = scratch[2 * DIFF_GROUP:3 * DIFF_GROUP]
    m_ref, l_ref, acc_ref = scratch[3 * DIFF_GROUP:]
    h = pl.program_id(1)
    i = pl.program_id(2)
    coef = coef_ref[h]
    q = q_ref[0]
    lane = lax.broadcasted_iota(jnp.int32, q.shape, 1)
    zero = jnp.zeros_like(q)
    qs = (jnp.where((lane < HEAD_DIM) | (lane >= DIFF_V), q, zero), jnp.where(lane >= HEAD_DIM, q, zero))

    m_ref[...] = jnp.full(m_ref.shape, MASK_VALUE, F32)
    l_ref[...] = jnp.zeros(l_ref.shape, F32)
    acc_ref[...] = jnp.zeros(acc_ref.shape, F32)

    def qk(j, s_ref):
        kt = k_ref[0, pl.ds(pl.multiple_of(j * tq, tq), tq), :]
        for c in range(2):
            s_ref[c] = _dot_nt(qs[c], kt)

    def softmax(j, s_ref, p_ref, alpha_ref, diag):
        base = coef * ((i - j) * tq).astype(F32)
        for c in range(2):
            for r in range(tq // ROW_BLOCK):
                rows = slice(r * ROW_BLOCK, (r + 1) * ROW_BLOCK)
                s = s_ref[c, rows, :]
                if diag:
                    col = lax.broadcasted_iota(jnp.int32, s.shape, 1)
                    row = r * ROW_BLOCK + lax.broadcasted_iota(jnp.int32, s.shape, 0)
                    s = jnp.where(col <= row, s, MASK_VALUE)
                m_old = m_ref[c, rows, :]
                m_new = jnp.maximum(m_old, jnp.max(s, axis=1, keepdims=True) - base)
                p = jnp.exp2(s - _lane_repeat(m_new + base, tq // 128))
                alpha = jnp.exp2(m_old - m_new)
                m_ref[c, rows, :] = m_new
                l_ref[c, rows, :] = (alpha * l_ref[c, rows, :]
                                     + sum(p[:, 128 * t:128 * (t + 1)] for t in range(tq // 128)))
                alpha_ref[c, rows, :] = alpha
                p_ref[c, rows, :] = p.astype(BF16)

    def pv(j, p_ref, alpha_ref):
        vt = v_ref[0, pl.ds(pl.multiple_of(j * tq, tq), tq), :]
        for c in range(2):
            acc_ref[c] = alpha_ref[c] * acc_ref[c] + _dot(p_ref[c], vt)

    def tile(j, r, diag):
        qk(j, s_refs[r])
        softmax(j, s_refs[r], p_refs[r], al_refs[r], diag)
        pv(j, p_refs[r], al_refs[r])

    tile(i, 0, True)

    b = pl.program_id(0)
    n_tiles = pl.num_programs(2)
    row0 = (b * DIFF_HEADS + h) * n_tiles
    k_off = pl.num_programs(0) * DIFF_HEADS * n_tiles
    qn = norm_ref[row0 + i]
    kn_diag = norm_ref[k_off + row0 + i]

    def leading_dead(j, first):
        gap = ((i - j - 1) * tq + 1).astype(F32)
        bound = qn * (norm_ref[k_off + row0 + j] + kn_diag) * NORM_SLACK + 1.0 - coef * gap
        return jnp.where((first == j) & (bound < -UNDERFLOW_LOG2), j + 1, first)

    first = lax.fori_loop(0, i, leading_dead, 0)
    group = len(s_refs)
    n_groups = (i - first) // group

    def body(jj, carry):
        for r in range(group):
            tile(first + group * jj + r, r, False)
        return carry

    lax.fori_loop(0, n_groups, body, 0)

    def rest(j, carry):
        tile(j, 0, False)
        return carry

    lax.fori_loop(first + n_groups * group, i, rest, 0)

    lv = lam_ref[...]
    lam = (jnp.exp(jnp.sum(lv[0:1] * lv[1:2], axis=1, keepdims=True))
           - jnp.exp(jnp.sum(lv[2:3] * lv[3:4], axis=1, keepdims=True)) + lambda_init)
    l0 = jnp.sum(l_ref[0], axis=1, keepdims=True)
    l1 = jnp.sum(l_ref[1], axis=1, keepdims=True)
    o = (acc_ref[0] * (1.0 / jnp.maximum(l0, 1e-30))
         - lam * (acc_ref[1] * (1.0 / jnp.maximum(l1, 1e-30))))
    o = _rms(o, sg_ref[...]) * (1.0 - lambda_init)
    o_ref[0] = o.astype(BF16)


def _diff_attn(coefs, norms, qx, kx, v, lam_vecs, subln_g, lambda_init):
    B, S, _ = v.shape
    H = DIFF_HEADS
    tq = POS_PERIOD
    kern = functools.partial(_diff_attn_kernel, tq=tq, lambda_init=lambda_init)
    return pl.pallas_call(
        kern,
        grid_spec=pltpu.PrefetchScalarGridSpec(
            num_scalar_prefetch=2,
            grid=(B, H, S // tq),
            in_specs=[pl.BlockSpec((1, tq, DIFF_QK), lambda b, h, i, *_: (b, i, h)),
                      pl.BlockSpec((1, S, DIFF_QK), lambda b, h, i, *_: (b, 0, h)),
                      pl.BlockSpec((1, S, DIFF_V), lambda b, h, i, *_: (b, 0, h)),
                      pl.BlockSpec((4, HEAD_DIM), lambda b, h, i, *_: (0, 0)),
                      pl.BlockSpec((1, DIFF_V), lambda b, h, i, *_: (0, 0))],
            out_specs=pl.BlockSpec((1, tq, DIFF_V), lambda b, h, i, *_: (b, i, h)),
            scratch_shapes=([pltpu.VMEM((2, tq, tq), F32)] * DIFF_GROUP
                            + [pltpu.VMEM((2, tq, tq), BF16)] * DIFF_GROUP
                            + [pltpu.VMEM((2, tq, 128), F32)] * DIFF_GROUP
                            + [pltpu.VMEM((2, tq, 128), F32), pltpu.VMEM((2, tq, 128), F32),
                               pltpu.VMEM((2, tq, DIFF_V), F32)])),
        out_shape=jax.ShapeDtypeStruct((B, S, H * DIFF_V), BF16),
        compiler_params=_cparams(3),
        name="diff_attn",
    )(coefs, norms, qx, kx, v, lam_vecs, subln_g)


def _ffn_kernel(x_ref, g_ref, wup_ref, cw_ref, cb_ref, wdn_ref, fg_ref, y_ref, ug_ref, uv_ref,
                *, tm, final_norm):
    i = pl.program_id(1)

    @pl.when(i == 0)
    def _():
        ug_ref[0:8, :] = jnp.zeros((8, D_FF), F32)
        uv_ref[0:8, :] = jnp.zeros((8, D_FF), F32)

    x = x_ref[0]
    h = _rms(x, g_ref[...]).astype(BF16)
    acc = jnp.zeros((tm, D_MODEL), F32)
    for c in range(D_FF // FF_CHUNK):
        halves = []
        for u_ref, col0 in ((ug_ref, FF_CHUNK * c), (uv_ref, D_FF + FF_CHUNK * c)):
            cols = slice(FF_CHUNK * c, FF_CHUNK * (c + 1))
            u_ref[8:8 + tm, cols] = _dot(h, wup_ref[:, col0:col0 + FF_CHUNK])
            w = cw_ref[:, col0:col0 + FF_CHUNK]
            y = (u_ref[6:6 + tm, cols] * w[0:1] + u_ref[7:7 + tm, cols] * w[1:2]
                 + u_ref[8:8 + tm, cols] * w[2:3] + cb_ref[:, col0:col0 + FF_CHUNK])
            u_ref[0:8, cols] = u_ref[tm:tm + 8, cols]
            halves.append(y)
        act = (jax.nn.silu(halves[0]) * halves[1]).astype(BF16)
        acc = acc + _dot(act, wdn_ref[FF_CHUNK * c:FF_CHUNK * (c + 1), :])
    out = x + acc
    if final_norm:
        out = _rms(out, fg_ref[...])
    y_ref[0] = out


def _ffn(x, g, w_up, conv_w, conv_b, w_down, final_g, final_norm, tm=512):
    B, S, D = x.shape
    kern = functools.partial(_ffn_kernel, tm=tm, final_norm=final_norm)
    const = lambda b, i: (0, 0)
    return pl.pallas_call(
        kern,
        grid=(B, S // tm),
        in_specs=[pl.BlockSpec((1, tm, D), lambda b, i: (b, i, 0)),
                  pl.BlockSpec((1, D), const),
                  pl.BlockSpec(w_up.shape, const, pipeline_mode=pl.Buffered(1)),
                  pl.BlockSpec(conv_w.shape, const),
                  pl.BlockSpec(conv_b.shape, const),
                  pl.BlockSpec(w_down.shape, const, pipeline_mode=pl.Buffered(1)),
                  pl.BlockSpec((1, D), const)],
        out_specs=pl.BlockSpec((1, tm, D), lambda b, i: (b, i, 0)),
        out_shape=jax.ShapeDtypeStruct((B, S, D), F32),
        scratch_shapes=[pltpu.VMEM((tm + 8, D_FF), F32), pltpu.VMEM((tm + 8, D_FF), F32)],
        compiler_params=_cparams(2),
        name="conv_ffn",
    )(x, g, w_up, conv_w, conv_b, w_down, final_g)


def _alibi_slopes(n):
    return jnp.exp2(-8.0 * jnp.arange(1, n + 1, dtype=F32) / n)


def _nsa_weight(w_in):
    gates = w_in[:, NSA_QKV_COLS:].reshape(D_MODEL, NSA_GROUPS, NSA_HPG, 3)
    gates = gates.transpose(0, 1, 3, 2).reshape(D_MODEL, NSA_GROUPS, 3 * NSA_HPG)
    gates = jnp.pad(gates, ((0, 0), (0, 0), (0, NSA_GATE_STRIDE - 3 * NSA_HPG)))
    gates = gates.reshape(D_MODEL, NSA_GROUPS * NSA_GATE_STRIDE)
    gates = jnp.pad(gates, ((0, 0), (0, NSA_GATE_PAD - NSA_GROUPS * NSA_GATE_STRIDE)))

    def widen(cols):
        w = w_in[:, cols].reshape(D_MODEL, -1, HEAD_DIM)
        return jnp.pad(w, ((0, 0), (0, 0), (0, NSA_ROW - HEAD_DIM))).reshape(D_MODEL, -1)

    nq, nk = NSA_HEADS * HEAD_DIM, NSA_GROUPS * HEAD_DIM
    kc, vc, ks, vs, kw, vw = [slice(nq + nk * b, nq + nk * (b + 1)) for b in range(6)]
    return jnp.concatenate([widen(slice(0, nq)), widen(ks), widen(kw),
                            w_in[:, kc], w_in[:, vc], w_in[:, vs], w_in[:, vw], gates], axis=1).astype(BF16)


def _coef_pieces(coefs):
    c_hi = coefs.astype(BF16)
    c_mid = (coefs - c_hi.astype(F32)).astype(BF16)
    c_lo = (coefs - c_hi.astype(F32) - c_mid.astype(F32)).astype(BF16)
    return jnp.stack([c_hi, c_hi, c_mid, c_mid, c_lo, c_lo], axis=1)


def _nsa_columns(coefs, nc):
    def place(cols, lane0):
        return jnp.pad(cols.astype(F32), ((0, 0), (lane0, NSA_ROW - lane0 - cols.shape[1])))

    q_cols = place(_coef_pieces(coefs), COL_COEF)
    pos = jnp.arange(SEL_CHUNK)
    k_pos = jnp.stack([pos - pos % 2, pos % 2] * 3, axis=1)
    k_mask = jnp.where(pos[:, None] // SEL_BLOCK == jnp.arange(BLOCKS_PER_CHUNK)[None, :], MASK_BIG, 0.0)
    k_cols = place(k_pos, COL_COEF) + place(k_mask, COL_MASK)
    j = jnp.arange(nc)
    c_pos = jnp.stack([4 * CMP_STRIDE * (j // 4), CMP_STRIDE * (j % 4)] * 3, axis=1)
    cmp_cols = jnp.stack([place(c_pos, COL_COEF), jnp.zeros((nc, NSA_ROW), F32)])
    return q_cols, k_cols, cmp_cols


def _nsa_layer(x, norm_g, w_in, k_pe, k_w1, k_w2, v_pe, v_w1, v_w2, w_out):
    B, S, _ = x.shape
    nc = S // CMP_STRIDE
    coefs = _alibi_slopes(NSA_HEADS) * LOG2E
    q_cols, k_cols, cmp_cols = _nsa_columns(coefs, nc)
    q, k128, kv, gates, nsq = _nsa_proj(x, norm_g[None], _nsa_weight(w_in), q_cols, k_cols)
    n_q = (SEL_CHUNK // NSA_TQ) * NSA_HEADS
    norms = jnp.sqrt(jnp.concatenate([nsq[:, :, 0, :n_q].reshape(-1),
                                      nsq[:, :, 0, n_q:n_q + NSA_GROUPS].reshape(-1)]))

    xc = kv[0:2].reshape(2, B, NSA_GROUPS, nc, CMP_STRIDE * HEAD_DIM)
    pe8 = jnp.broadcast_to(jnp.stack([k_pe, v_pe]).reshape(2, 1, CMP_BLOCK * HEAD_DIM),
                           (2, 8, CMP_BLOCK * HEAD_DIM)).astype(BF16)
    w2 = jnp.pad(jnp.stack([k_w2, v_w2]), ((0, 0), (0, 0), (0, NSA_ROW - HEAD_DIM))).astype(BF16)
    cmp = _compress(xc, pe8, jnp.stack([k_w1, v_w1]).astype(BF16), w2, cmp_cols)
    kcmp, vcmp_t = cmp[0], cmp[1][..., :HEAD_DIM].transpose(0, 1, 3, 2)

    nsb = S // SEL_BLOCK
    ratio = SEL_BLOCK // CMP_STRIDE
    sb = jnp.arange(nsb)[:, None]
    cb = jnp.arange(nc)[None, :]
    pool_t = ((cb >= ratio * sb - 1) & (cb <= ratio * sb + ratio - 1)).astype(BF16)
    grp = (jnp.arange(nsb)[:, None] // BLOCKS_PER_CHUNK == jnp.arange(128)[None, :]).astype(BF16)
    ocmp, nsel, bits = _cmp_topk(q, kcmp, vcmp_t, pool_t, grp)
    sub = CMP_TQ // NSA_TQ
    word = (bits[:, 0, 0:2 * sub:2] | (bits[:, 0, 1:2 * sub:2] << 16)).reshape(-1)
    o = _sel_win(word, coefs, norms, q, k128, kv, nsel, ocmp, gates)
    return _out_proj(o, w_out.astype(BF16), x)


def _alibi_columns(coefs, period):
    pieces = _coef_pieces(coefs)
    pos = jnp.arange(period)
    parts = jnp.stack([pos - pos % 2, pos % 2] * 3, axis=1).astype(BF16)
    n = coefs.shape[0]
    pieces_b = jnp.broadcast_to(pieces[None], (period, n, 6))
    parts_b = jnp.broadcast_to(parts[:, None], (period, n, 6))
    pad = jnp.zeros((period, n, DIFF_V - 12), BF16)
    eq = jnp.concatenate([pieces_b, parts_b, pad], axis=2).reshape(period, n * DIFF_V)
    ek = jnp.concatenate([parts_b, -pieces_b, pad], axis=2).reshape(period, n * DIFF_V)
    return eq, ek


def _diff_layer(x, norm_g, w_in, lq1, lk1, lq2, lk2, subln_g, w_out, lambda_init):
    nqk = DIFF_HEADS * DIFF_V
    scale = jnp.concatenate([jnp.full((nqk,), HEAD_DIM ** -0.5 * LOG2E, F32), jnp.ones((2 * nqk,), F32)])
    coefs = _alibi_slopes(DIFF_HEADS) * LOG2E
    eq, ek = _alibi_columns(coefs, POS_PERIOD)
    qx, kx, v, nsq = _diff_proj(x, norm_g[None], (w_in * scale[None, :]).astype(BF16), eq, ek)
    nsq = nsq[:, :, 0, :2 * DIFF_HEADS].reshape(nsq.shape[0], nsq.shape[1], 2, DIFF_HEADS)
    norms = jnp.sqrt(nsq).transpose(2, 0, 3, 1).reshape(-1)
    lam_vecs = jnp.stack([lq1, lk1, lq2, lk2]).astype(F32)
    o = _diff_attn(coefs, norms, qx, kx, v, lam_vecs, subln_g[None].astype(F32), lambda_init)
    return _out_proj(o, w_out.astype(BF16), x)


def kernel(x, norm_mix_g, norm_ffn_g, final_norm_g, nsa_w_in, nsa_cmp_k_pe, nsa_cmp_k_w1, nsa_cmp_k_w2, nsa_cmp_v_pe, nsa_cmp_v_w1, nsa_cmp_v_w2, nsa_w_out, diff_w_in, diff_lam_q1, diff_lam_k1, diff_lam_q2, diff_lam_k2, diff_subln_g, diff_w_out, ffn_w_up, ffn_conv_w, ffn_conv_b, ffn_w_down):
    depth = norm_mix_g.shape[0]
    for i in range(depth):
        j = i // 2
        if i % 2 == 0:
            x = _nsa_layer(x, norm_mix_g[i], nsa_w_in[j], nsa_cmp_k_pe[j], nsa_cmp_k_w1[j], nsa_cmp_k_w2[j],
                           nsa_cmp_v_pe[j], nsa_cmp_v_w1[j], nsa_cmp_v_w2[j], nsa_w_out[j])
        else:
            lambda_init = 0.8 - 0.6 * math.exp(-0.3 * i)
            x = _diff_layer(x, norm_mix_g[i], diff_w_in[j], diff_lam_q1[j], diff_lam_k1[j], diff_lam_q2[j],
                            diff_lam_k2[j], diff_subln_g[j], diff_w_out[j], lambda_init)
        x = _ffn(x, norm_ffn_g[i][None], ffn_w_up[i].astype(BF16), ffn_conv_w[i], ffn_conv_b[i][None],
                 ffn_w_down[i].astype(BF16), final_norm_g[None], final_norm=(i == depth - 1))
    return x
```

```python
import functools
import math

import jax
import jax.numpy as jnp
from jax import lax
from jax.experimental import pallas as pl
from jax.experimental.pallas import tpu as pltpu

F32 = jnp.float32
BF16 = jnp.bfloat16

D_MODEL = 1024
EPS = 1e-6
MASK_VALUE = -1e30
FORCED_SCORE = 1e6

NSA_HEADS = 16
NSA_GROUPS = 4
NSA_HPG = 4
HEAD_DIM = 64
CMP_STRIDE = 16
CMP_BLOCK = 32
CMP_HIDDEN = 256
SEL_BLOCK = 64
SEL_TOP_N = 16
WINDOW = 512
NSA_QKV_COLS = NSA_HEADS * HEAD_DIM + 6 * NSA_GROUPS * HEAD_DIM
NSA_GATE_PAD = 128
NSA_GATE_STRIDE = 16

DIFF_HEADS = 8
DIFF_V = 128
DIFF_QK = 256
POS_PERIOD = 512
ROW_BLOCK = 32
DIFF_GROUP = 3
LOG2E = 1.4426950408889634
UNDERFLOW_LOG2 = 160.0
NORM_SLACK = 1.01

D_FF = 2816
FF_CHUNK = 256

SEL_CHUNK = 512
BLOCKS_PER_CHUNK = SEL_CHUNK // SEL_BLOCK
MASK_BIG = -(2.0 ** 100)
NO_KEY_FLOOR = -1e29
NSA_ROW = 128
COL_COEF = 64
COL_MASK = 70
CMP_CHUNK = 256
NSA_TQ = 256
CMP_TQ = 512

VMEM_LIMIT = 56 * 1024 * 1024


def _cparams(n_axes):
    return pltpu.CompilerParams(dimension_semantics=("arbitrary",) * n_axes,
                                vmem_limit_bytes=VMEM_LIMIT)


def _rms(x, g):
    return x * lax.rsqrt(jnp.mean(x * x, axis=-1, keepdims=True) + EPS) * g


def _dot(a, b):
    return jnp.dot(a, b, preferred_element_type=F32)


def _lane_repeat(x, n):
    return jnp.concatenate([x] * n, axis=1)


def _dot_nt(a, b):
    return lax.dot_general(a, b, (((1,), (1,)), ((), ())), preferred_element_type=F32)


def _nsa_proj_kernel(x_ref, g_ref, w_ref, qc_ref, kc_ref, q_ref, k_ref, kv_ref, gate_ref, n_ref):
    h = _rms(x_ref[0], g_ref[...]).astype(BF16)
    tm = h.shape[0]
    lane = lax.broadcasted_iota(jnp.int32, (8, 128), 1)
    norms = jnp.zeros((8, 128), F32)
    col = 0
    for pair in range(NSA_HEADS // 2):
        pc = _dot(h, w_ref[:, col:col + 2 * NSA_ROW]) * (HEAD_DIM ** -0.5 * LOG2E)
        for s in range(2):
            hd = 2 * pair + s
            qh = pc[:, NSA_ROW * s:NSA_ROW * (s + 1)].astype(BF16)
            q_ref[0, hd] = qh + qc_ref[hd:hd + 1, :].astype(BF16)
            for sub in range(tm // NSA_TQ):
                part = _max_row_norm_sq(qh[sub * NSA_TQ:(sub + 1) * NSA_TQ])
                norms = jnp.where(lane == sub * NSA_HEADS + hd, part, norms)
        col += 2 * NSA_ROW
    for br in range(2):
        for pair in range(NSA_GROUPS // 2):
            pc = _dot(h, w_ref[:, col:col + 2 * NSA_ROW])
            for s in range(2):
                piece = pc[:, NSA_ROW * s:NSA_ROW * (s + 1)].astype(BF16)
                if br == 0:
                    k_ref[br, 0, 2 * pair + s] = piece + kc_ref[...].astype(BF16)
                    norms = jnp.where(lane == (tm // NSA_TQ) * NSA_HEADS + 2 * pair + s,
                                      _max_row_norm_sq(piece), norms)
                else:
                    k_ref[br, 0, 2 * pair + s] = piece
            col += 2 * NSA_ROW
    n_ref[0, 0] = norms
    for br in range(4):
        pc = _dot(h, w_ref[:, col:col + 256])
        for g in range(NSA_GROUPS):
            kv_ref[br, 0, g] = pc[:, 64 * g:64 * (g + 1)].astype(BF16)
        col += 256
    pg = _dot(h, w_ref[:, col:col + NSA_GATE_PAD])
    for g in range(NSA_GROUPS):
        gate_ref[0, g] = jax.nn.sigmoid(pg[:, NSA_GATE_STRIDE * g:NSA_GATE_STRIDE * (g + 1)])


def _nsa_proj(x, g, w, q_cols, k_cols):
    B, S, D = x.shape
    tm = SEL_CHUNK
    n = w.shape[1]
    return pl.pallas_call(
        _nsa_proj_kernel,
        grid=(B, S // tm),
        in_specs=[pl.BlockSpec((1, tm, D), lambda b, i: (b, i, 0)),
                  pl.BlockSpec((1, D), lambda b, i: (0, 0)),
                  pl.BlockSpec((D, n), lambda b, i: (0, 0)),
                  pl.BlockSpec(q_cols.shape, lambda b, i: (0, 0)),
                  pl.BlockSpec(k_cols.shape, lambda b, i: (0, 0))],
        out_specs=[pl.BlockSpec((1, NSA_HEADS, tm, NSA_ROW), lambda b, i: (b, 0, i, 0)),
                   pl.BlockSpec((2, 1, NSA_GROUPS, tm, NSA_ROW), lambda b, i: (0, b, 0, i, 0)),
                   pl.BlockSpec((4, 1, NSA_GROUPS, tm, HEAD_DIM), lambda b, i: (0, b, 0, i, 0)),
                   pl.BlockSpec((1, NSA_GROUPS, tm, NSA_GATE_STRIDE), lambda b, i: (b, 0, i, 0)),
                   pl.BlockSpec((1, 1, 8, 128), lambda b, i: (b, i, 0, 0))],
        out_shape=[jax.ShapeDtypeStruct((B, NSA_HEADS, S, NSA_ROW), BF16),
                   jax.ShapeDtypeStruct((2, B, NSA_GROUPS, S, NSA_ROW), BF16),
                   jax.ShapeDtypeStruct((4, B, NSA_GROUPS, S, HEAD_DIM), BF16),
                   jax.ShapeDtypeStruct((B, NSA_GROUPS, S, NSA_GATE_STRIDE), F32),
                   jax.ShapeDtypeStruct((B, S // tm, 8, 128), F32)],
        compiler_params=_cparams(2),
        name="nsa_proj",
    )(x, g, w, q_cols, k_cols)


def _compress_kernel(x_ref, pe_ref, w1_ref, w2_ref, cols_ref, o_ref, shift_ref):
    nc = x_ref.shape[3]
    half = CMP_STRIDE * HEAD_DIM
    x = x_ref[0, 0, 0]
    first = _dot(x, w1_ref[0, :half, :])
    second = _dot(x, w1_ref[0, half:, :])
    shift_ref[0:nc, :] = second
    shift_ref[nc:nc + 8, :] = jnp.zeros((8, CMP_HIDDEN), F32)
    pe_term = _dot(pe_ref[0], w1_ref[0])[0:1, :]
    pre = first + shift_ref[pl.ds(1, nc), :] + pe_term
    hid = jax.nn.gelu(pre, approximate=True)
    o_ref[0, 0, 0] = (_dot(hid.astype(BF16), w2_ref[0]) + cols_ref[0]).astype(BF16)


def _compress(xc, pe8, w1, w2, cols):
    _, B, G, nc, width = xc.shape
    return pl.pallas_call(
        _compress_kernel,
        grid=(2, B, G),
        in_specs=[pl.BlockSpec((1, 1, 1, nc, width), lambda a, b, g: (a, b, g, 0, 0)),
                  pl.BlockSpec((1, 8, CMP_BLOCK * HEAD_DIM), lambda a, b, g: (a, 0, 0)),
                  pl.BlockSpec((1, CMP_BLOCK * HEAD_DIM, CMP_HIDDEN), lambda a, b, g: (a, 0, 0)),
                  pl.BlockSpec((1, CMP_HIDDEN, NSA_ROW), lambda a, b, g: (a, 0, 0)),
                  pl.BlockSpec((1, nc, NSA_ROW), lambda a, b, g: (a, 0, 0))],
        out_specs=pl.BlockSpec((1, 1, 1, nc, NSA_ROW), lambda a, b, g: (a, b, g, 0, 0)),
        out_shape=jax.ShapeDtypeStruct((2, B, G, nc, NSA_ROW), BF16),
        scratch_shapes=[pltpu.VMEM((nc + 8, CMP_HIDDEN), F32)],
        compiler_params=_cparams(3),
        name="nsa_compress",
    )(xc, pe8, w1, w2, cols)


def _cmp_topk_kernel(q_ref, kc_ref, vct_ref, pool_ref, grp_ref, ocmp_ref, nsel_ref, bits_ref, *, tq):
    i = pl.program_id(2)
    ncp = kc_ref.shape[2]
    nsel_blocks = pool_ref.shape[0]
    t0 = i * tq
    ratio = SEL_BLOCK // CMP_STRIDE

    def visible_prefix(k, three_forced):
        rows = CMP_CHUNK * k
        nsb = rows // ratio
        kc = kc_ref[0, 0, 0:rows, :]
        vct = vct_ref[0, 0, :, 0:rows]
        tok = t0 + lax.broadcasted_iota(jnp.int32, (rows, tq), 1)
        cmp_end = lax.broadcasted_iota(jnp.int32, (rows, tq), 0) * CMP_STRIDE + (CMP_BLOCK - 1)
        mask = tok >= cmp_end

        p_grp = jnp.zeros((rows, tq), F32)
        for p in range(NSA_HPG):
            s = jnp.where(mask, _dot_nt(kc, q_ref[0, p]), MASK_VALUE)
            m = jnp.maximum(jnp.max(s, axis=0, keepdims=True), NO_KEY_FLOOR)
            e = jnp.exp2(s - m)
            r = 1.0 / jnp.maximum(jnp.sum(e, axis=0, keepdims=True), 1e-30)
            p_grp = p_grp + e * r
            ocmp_ref[0, p] = (_dot(vct, e.astype(BF16)) * r).T

        hi = p_grp.astype(BF16)
        r1 = p_grp - hi.astype(F32)
        mid = r1.astype(BF16)
        lo = (r1 - mid.astype(F32)).astype(BF16)
        pool = pool_ref[0:nsb, 0:rows]
        imp = _dot(pool, hi) + _dot(pool, mid) + _dot(pool, lo)

        blk = lax.broadcasted_iota(jnp.int32, (nsb, tq), 0)
        cur = (t0 + lax.broadcasted_iota(jnp.int32, (nsb, tq), 1)) // SEL_BLOCK
        valid = blk <= cur
        forced = (blk == 0) | (valid & (blk >= cur - 1))
        blk_f = blk.astype(F32)
        if three_forced:
            work = jnp.where(forced, -jnp.inf, jnp.where(valid, imp, -1.0))
            sel = jnp.where(forced, 1.0, 0.0)
            rounds = min(SEL_TOP_N, nsb) - 3
        else:
            work = jnp.where(forced, FORCED_SCORE, jnp.where(valid, imp, -1.0))
            sel = jnp.zeros((nsb, tq), F32)
            rounds = min(SEL_TOP_N, nsb)
        for _ in range(rounds):
            mx = jnp.max(work, axis=0, keepdims=True)
            first = jnp.min(jnp.where(work == mx, blk_f, float(nsb)), axis=0, keepdims=True)
            pick = blk_f == first
            sel = jnp.where(pick, 1.0, sel)
            work = jnp.where(pick, -jnp.inf, work)
        sel = jnp.where(valid, sel, 0.0)
        if nsb < nsel_blocks:
            sel = jnp.concatenate([sel, jnp.zeros((nsel_blocks - nsb, tq), F32)], axis=0)

        sel_t = sel.T
        nsel_ref[0, 0] = (1.0 - sel_t).astype(BF16)

        lane = lax.broadcasted_iota(jnp.int32, (1, 128), 1)
        bit = jnp.left_shift(1, lane & 15).astype(F32)
        words = jnp.zeros((1, 128), F32)
        for sub in range(tq // NSA_TQ):
            any_blk = jnp.max(sel_t[sub * NSA_TQ:(sub + 1) * NSA_TQ], axis=0, keepdims=True)
            cnt = _dot(jnp.broadcast_to(any_blk, (8, nsel_blocks)).astype(BF16), grp_ref[...])[0:1, :]
            flag = cnt > 0.0
            lo_word = jnp.sum(jnp.where(flag & (lane < 16), bit, 0.0), axis=1, keepdims=True)
            hi_word = jnp.sum(jnp.where(flag & (lane >= 16) & (lane < 32), bit, 0.0), axis=1, keepdims=True)
            words = jnp.where(lane == 2 * sub, lo_word, jnp.where(lane == 2 * sub + 1, hi_word, words))
        bits_ref[0] = words.astype(jnp.int32)

    n_chunks = ncp // CMP_CHUNK
    last_visible = (t0 + tq - CMP_BLOCK) // CMP_STRIDE
    assert tq >= 2 * SEL_BLOCK
    variant = jnp.where(i == 0, 0, 1 + jnp.clip(last_visible // CMP_CHUNK, 0, n_chunks - 1))
    lax.switch(variant, [functools.partial(visible_prefix, 1, False)]
               + [functools.partial(visible_prefix, k, True) for k in range(1, n_chunks + 1)])


def _cmp_topk(q, kcmp, vcmp_t, pool_t, grp, tq=CMP_TQ):
    B, H, S, _ = q.shape
    G = NSA_GROUPS
    ncp = kcmp.shape[2]
    nsb = pool_t.shape[0]
    nq = S // tq
    kern = functools.partial(_cmp_topk_kernel, tq=tq)
    return pl.pallas_call(
        kern,
        grid=(B, G, nq),
        in_specs=[pl.BlockSpec((1, NSA_HPG, tq, NSA_ROW), lambda b, g, i: (b, g, i, 0)),
                  pl.BlockSpec((1, 1, ncp, NSA_ROW), lambda b, g, i: (b, g, 0, 0)),
                  pl.BlockSpec((1, 1, HEAD_DIM, ncp), lambda b, g, i: (b, g, 0, 0)),
                  pl.BlockSpec((nsb, ncp), lambda b, g, i: (0, 0)),
                  pl.BlockSpec((nsb, 128), lambda b, g, i: (0, 0))],
        out_specs=[pl.BlockSpec((1, NSA_HPG, tq, HEAD_DIM), lambda b, g, i: (b, g, i, 0)),
                   pl.BlockSpec((1, 1, tq, nsb), lambda b, g, i: (b, g, i, 0)),
                   pl.BlockSpec((1, 1, 128), lambda b, g, i: ((b * G + g) * nq + i, 0, 0))],
        out_shape=[jax.ShapeDtypeStruct((B, H, S, HEAD_DIM), F32),
                   jax.ShapeDtypeStruct((B, G, S, nsb), BF16),
                   jax.ShapeDtypeStruct((B * G * nq, 1, 128), jnp.int32)],
        compiler_params=_cparams(3),
        name="nsa_cmp_topk",
    )(q, kcmp, vcmp_t, pool_t, grp)


def _sel_win_kernel(bits_ref, coef_ref, norm_ref, q_ref, ks_ref, kw_ref, vs_ref, vw_ref, nsel_ref,
                    ocmp_ref, gate_ref, o_ref, s_ref, p_ref, alpha_ref, m_ref, l_ref, acc_ref, *, tq):
    b = pl.program_id(0)
    g = pl.program_id(1)
    i = pl.program_id(2)
    nq = pl.num_programs(2)
    nsb = nsel_ref.shape[3]
    rows = NSA_HPG * tq
    t0 = i * tq
    q4 = q_ref[0].reshape(rows, NSA_ROW)
    word = bits_ref[((b * NSA_GROUPS + g) * nq + i)]
    coefs = [coef_ref[NSA_HPG * g + p] for p in range(NSA_HPG)]
    coef_rows = jnp.concatenate([jnp.full((tq, 128), c, F32) for c in coefs], axis=0)
    diag_chunk = t0 // SEL_CHUNK

    n_chunks = nq * tq // SEL_CHUNK
    q_row = (b * nq + i) * NSA_HEADS + NSA_HPG * g
    k_off = pl.num_programs(0) * nq * NSA_HEADS
    qn = functools.reduce(jnp.maximum, [norm_ref[q_row + p] for p in range(NSA_HPG)])
    coef_min = functools.reduce(jnp.minimum, coefs)
    kn_diag = norm_ref[k_off + (b * n_chunks + diag_chunk) * NSA_GROUPS + g]

    def prune(c, w):
        gap = (t0 - (c * SEL_CHUNK + SEL_CHUNK - 1)).astype(F32)
        kn = norm_ref[k_off + (b * n_chunks + c) * NSA_GROUPS + g]
        bound = qn * (kn + kn_diag) * NORM_SLACK + 1.0 - coef_min * gap
        return jnp.where(bound < -UNDERFLOW_LOG2, w & ~jnp.left_shift(1, c), w)

    word = lax.fori_loop(0, diag_chunk, prune, word)

    m_ref[...] = jnp.full(m_ref.shape, MASK_VALUE, F32)
    l_ref[...] = jnp.zeros(l_ref.shape, F32)
    acc_ref[...] = jnp.zeros(acc_ref.shape, F32)
    nsel = nsel_ref[0, 0].astype(F32)
    lane = lax.broadcasted_iota(jnp.int32, (tq, NSA_ROW), 1)
    flag_lanes = (lane >= COL_MASK) & (lane < COL_MASK + BLOCKS_PER_CHUNK)
    diag_chunk = t0 // SEL_CHUNK

    def chunk_step(c, causal):
        k0 = pl.multiple_of(c * SEL_CHUNK, SEL_CHUNK)
        kc = ks_ref[0, 0, pl.ds(k0, SEL_CHUNK), :]
        vc = vs_ref[0, 0, pl.ds(k0, SEL_CHUNK), :]
        shift = (COL_MASK + nsb - BLOCKS_PER_CHUNK * c) % nsb
        flags = jnp.where(flag_lanes, pltpu.roll(nsel, shift, 1)[:, :NSA_ROW], 0.0).astype(BF16)
        s_ref[...] = _dot_nt(q4 + jnp.tile(flags, (NSA_HPG, 1)), kc)
        base = coef_rows * (k0 - t0).astype(F32)
        for r in range(rows // ROW_BLOCK):
            rr = slice(r * ROW_BLOCK, (r + 1) * ROW_BLOCK)
            s = s_ref[rr, :]
            if causal:
                col = lax.broadcasted_iota(jnp.int32, s.shape, 1)
                row = (r * ROW_BLOCK) % tq + lax.broadcasted_iota(jnp.int32, s.shape, 0)
                s = jnp.where(col - row <= t0 - k0, s, MASK_VALUE)
            m_old = m_ref[rr, :]
            m_new = jnp.maximum(m_old, jnp.max(s, axis=1, keepdims=True) + base[rr])
            p = jnp.exp2(s - _lane_repeat(m_new - base[rr], SEL_CHUNK // 128))
            alpha = jnp.exp2(m_old - m_new)
            m_ref[rr, :] = m_new
            l_ref[rr, :] = alpha * l_ref[rr, :] + sum(p[:, 128 * t:128 * (t + 1)] for t in range(SEL_CHUNK // 128))
            alpha_ref[rr, :] = alpha
            p_ref[rr, :] = p.astype(BF16)
        acc_ref[...] = alpha_ref[:, 0:HEAD_DIM] * acc_ref[...] + _dot(p_ref[...], vc)

    def body(c, carry):
        @pl.when(((word >> c) & 1) == 1)
        def _():
            chunk_step(c, False)
        return carry

    lax.fori_loop(0, diag_chunk, body, 0)
    chunk_step(diag_chunk, True)
    l_sel = jnp.sum(l_ref[...], axis=1, keepdims=True)
    o_sel = acc_ref[...] * (1.0 / jnp.maximum(l_sel, 1e-30))

    wlen = WINDOW + tq
    ws = pl.multiple_of(jnp.maximum(t0 - WINDOW, 0), tq)
    kwin = kw_ref[0, 0, pl.ds(ws, wlen), :]
    vwin = vw_ref[0, 0, pl.ds(ws, wlen), :]
    dist = ((t0 - ws) + lax.broadcasted_iota(jnp.int32, (tq, wlen), 0)
            - lax.broadcasted_iota(jnp.int32, (tq, wlen), 1))
    wmask = (dist >= 0) & (dist < WINDOW)
    dist_f = dist.astype(F32)
    wbias = jnp.concatenate([jnp.where(wmask, -c * dist_f, MASK_VALUE) for c in coefs], axis=0)
    sw = _dot_nt(q4, kwin) + wbias
    ew = jnp.exp2(sw - jnp.max(sw, axis=1, keepdims=True))
    lw = jnp.sum(ew, axis=1, keepdims=True)
    o_win = _dot(ew.astype(BF16), vwin) * (1.0 / jnp.maximum(lw, 1e-30))

    gt = gate_ref[0, 0]
    outs = []
    for p in range(NSA_HPG):
        r = slice(p * tq, (p + 1) * tq)
        o = (gt[:, p:p + 1] * ocmp_ref[0, p]
             + gt[:, 4 + p:5 + p] * o_sel[r]
             + gt[:, 8 + p:9 + p] * o_win[r])
        outs.append(o)
    o_ref[0] = jnp.concatenate(outs, axis=1).astype(BF16)


def _sel_win(bits, coefs, norms, q, k128, kv, nsel, ocmp, gates, tq=NSA_TQ):
    B, H, S, _ = q.shape
    G = NSA_GROUPS
    nsb = nsel.shape[3]
    nq = S // tq
    rows = NSA_HPG * tq
    kern = functools.partial(_sel_win_kernel, tq=tq)

    def resident(branch, width):
        return pl.BlockSpec((1, 1, 1, S, width), lambda b, g, i, *_: (branch, b, g, 0, 0),
                            pipeline_mode=pl.Buffered(1))

    def kv_kernel(bits_ref, coef_ref, norm_ref, q_ref, ks_ref, kw_ref, vs_ref, vw_ref, *rest):
        kern(bits_ref, coef_ref, norm_ref, q_ref, ks_ref.at[0], kw_ref.at[0], vs_ref.at[0], vw_ref.at[0],
             *rest)

    return pl.pallas_call(
        kv_kernel,
        grid_spec=pltpu.PrefetchScalarGridSpec(
            num_scalar_prefetch=3,
            grid=(B, G, nq),
            in_specs=[pl.BlockSpec((1, NSA_HPG, tq, NSA_ROW), lambda b, g, i, *_: (b, g, i, 0)),
                      resident(0, NSA_ROW), resident(1, NSA_ROW),
                      resident(2, HEAD_DIM), resident(3, HEAD_DIM),
                      pl.BlockSpec((1, 1, tq, nsb), lambda b, g, i, *_: (b, g, i, 0)),
                      pl.BlockSpec((1, NSA_HPG, tq, HEAD_DIM), lambda b, g, i, *_: (b, g, i, 0)),
                      pl.BlockSpec((1, 1, tq, NSA_GATE_STRIDE), lambda b, g, i, *_: (b, g, i, 0))],
            out_specs=pl.BlockSpec((1, tq, NSA_HPG * HEAD_DIM), lambda b, g, i, *_: (b, i, g)),
            scratch_shapes=[pltpu.VMEM((rows, SEL_CHUNK), F32), pltpu.VMEM((rows, SEL_CHUNK), BF16),
                            pltpu.VMEM((rows, 128), F32), pltpu.VMEM((rows, 128), F32),
                            pltpu.VMEM((rows, 128), F32), pltpu.VMEM((rows, HEAD_DIM), F32)]),
        out_shape=jax.ShapeDtypeStruct((B, S, H * HEAD_DIM), BF16),
        compiler_params=_cparams(3),
        name="nsa_sel_win",
    )(bits, coefs, norms, q, k128, k128, kv, kv, nsel, ocmp, gates)


def _max_row_norm_sq(x):
    xf = x.astype(F32)
    return jnp.max(jnp.sum(xf * xf, axis=1, keepdims=True), axis=0, keepdims=True)


def _diff_proj_kernel(x_ref, g_ref, w_ref, eq_ref, ek_ref, q_ref, k_ref, v_ref, n_ref):
    h = _rms(x_ref[0], g_ref[...]).astype(BF16)
    nqk = DIFF_HEADS * DIFF_V
    lane = lax.broadcasted_iota(jnp.int32, (8, 128), 1)
    norms = jnp.zeros((8, 128), F32)
    for hd in range(DIFF_HEADS):
        lo, mid, hi = DIFF_QK * hd, DIFF_QK * hd + DIFF_V, DIFF_QK * (hd + 1)
        qh = _dot(h, w_ref[:, DIFF_V * hd:DIFF_V * (hd + 1)]).astype(BF16)
        kh = _dot(h, w_ref[:, nqk + DIFF_V * hd:nqk + DIFF_V * (hd + 1)]).astype(BF16)
        q_ref[0, :, lo:mid] = qh
        q_ref[0, :, mid:hi] = eq_ref[:, DIFF_V * hd:DIFF_V * (hd + 1)]
        k_ref[0, :, lo:mid] = kh
        k_ref[0, :, mid:hi] = ek_ref[:, DIFF_V * hd:DIFF_V * (hd + 1)]
        norms = jnp.where(lane == hd, _max_row_norm_sq(qh), norms)
        norms = jnp.where(lane == DIFF_HEADS + hd, _max_row_norm_sq(kh), norms)
    n_ref[0, 0] = norms
    for c in range(nqk // 512):
        v_ref[0, :, 512 * c:512 * (c + 1)] = _dot(h, w_ref[:, 2 * nqk + 512 * c:2 * nqk + 512 * (c + 1)]).astype(BF16)


def _diff_proj(x, g, w, eq, ek):
    B, S, D = x.shape
    tm = POS_PERIOD
    nqk = DIFF_HEADS * DIFF_QK
    nv = DIFF_HEADS * DIFF_V
    row = lambda b, i: (b, i, 0)
    const = lambda b, i: (0, 0)
    return pl.pallas_call(
        _diff_proj_kernel,
        grid=(B, S // tm),
        in_specs=[pl.BlockSpec((1, tm, D), row),
                  pl.BlockSpec((1, D), const),
                  pl.BlockSpec(w.shape, const),
                  pl.BlockSpec(eq.shape, const),
                  pl.BlockSpec(ek.shape, const)],
        out_specs=[pl.BlockSpec((1, tm, nqk), row), pl.BlockSpec((1, tm, nqk), row),
                   pl.BlockSpec((1, tm, nv), row),
                   pl.BlockSpec((1, 1, 8, 128), lambda b, i: (b, i, 0, 0))],
        out_shape=[jax.ShapeDtypeStruct((B, S, nqk), BF16), jax.ShapeDtypeStruct((B, S, nqk), BF16),
                   jax.ShapeDtypeStruct((B, S, nv), BF16),
                   jax.ShapeDtypeStruct((B, S // tm, 8, 128), F32)],
        compiler_params=_cparams(2),
        name="diff_proj",
    )(x, g, w, eq, ek)


def _diff_attn_kernel(coef_ref, norm_ref, q_ref, k_ref, v_ref, lam_ref, sg_ref, o_ref,
                      *scratch, tq, lambda_init):
    s_refs = scratch[0:DIFF_GROUP]
    p_refs = scratch[DIFF_GROUP:2 * DIFF_GROUP]
    al_refs = scratch[2 * DIFF_GROUP:3 * DIFF_GROUP]
    m_ref, l_ref, acc_ref = scratch[3 * DIFF_GROUP:]
    h = pl.program_id(1)
    i = pl.program_id(2)
    coef = coef_ref[h]
    q = q_ref[0]
    lane = lax.broadcasted_iota(jnp.int32, q.shape, 1)
    zero = jnp.zeros_like(q)
    qs = (jnp.where((lane < HEAD_DIM) | (lane >= DIFF_V), q, zero), jnp.where(lane >= HEAD_DIM, q, zero))

    m_ref[...] = jnp.full(m_ref.shape, MASK_VALUE, F32)
    l_ref[...] = jnp.zeros(l_ref.shape, F32)
    acc_ref[...] = jnp.zeros(acc_ref.shape, F32)

    def qk(j, s_ref):
        kt = k_ref[0, pl.ds(pl.multiple_of(j * tq, tq), tq), :]
        for c in range(2):
            s_ref[c] = _dot_nt(qs[c], kt)

    def softmax(j, s_ref, p_ref, alpha_ref, diag):
        base = coef * ((i - j) * tq).astype(F32)
        for c in range(2):
            for r in range(tq // ROW_BLOCK):
                rows = slice(r * ROW_BLOCK, (r + 1) * ROW_BLOCK)
                s = s_ref[c, rows, :]
                if diag:
                    col = lax.broadcasted_iota(jnp.int32, s.shape, 1)
                    row = r * ROW_BLOCK + lax.broadcasted_iota(jnp.int32, s.shape, 0)
                    s = jnp.where(col <= row, s, MASK_VALUE)
                m_old = m_ref[c, rows, :]
                m_new = jnp.maximum(m_old, jnp.max(s, axis=1, keepdims=True) - base)
                p = jnp.exp2(s - _lane_repeat(m_new + base, tq // 128))
                alpha = jnp.exp2(m_old - m_new)
                m_ref[c, rows, :] = m_new
                l_ref[c, rows, :] = (alpha * l_ref[c, rows, :]
                                     + sum(p[:, 128 * t:128 * (t + 1)] for t in range(tq // 128)))
                alpha_ref[c, rows, :] = alpha
                p_ref[c, rows, :] = p.astype(BF16)

    def pv(j, p_ref, alpha_ref):
        vt = v_ref[0, pl.ds(pl.multiple_of(j * tq, tq), tq), :]
        for c in range(2):
            acc_ref[c] = alpha_ref[c] * acc_ref[c] + _dot(p_ref[c], vt)

    def tile(j, r, diag):
        qk(j, s_refs[r])
        softmax(j, s_refs[r], p_refs[r], al_refs[r], diag)
        pv(j, p_refs[r], al_refs[r])


    b = pl.program_id(0)
    n_tiles = pl.num_programs(2)
    row0 = (b * DIFF_HEADS + h) * n_tiles
    k_off = pl.num_programs(0) * DIFF_HEADS * n_tiles
    qn = norm_ref[row0 + i]
    kn_diag = norm_ref[k_off + row0 + i]

    def leading_dead(j, first):
        gap = ((i - j - 1) * tq + 1).astype(F32)
        bound = qn * (norm_ref[k_off + row0 + j] + kn_diag) * NORM_SLACK + 1.0 - coef * gap
        return jnp.where((first == j) & (bound < -UNDERFLOW_LOG2), j + 1, first)

    first = lax.fori_loop(0, i, leading_dead, 0)
    group = len(s_refs)
    lead = jnp.where(i - first >= group - 1, group - 1, 0)

    @pl.when(lead > 0)
    def _():
        tile(i, 0, True)
        for r in range(1, group):
            tile(first + r - 1, r, False)

    @pl.when(lead == 0)
    def _():
        tile(i, 0, True)

    start = first + lead
    n_groups = (i - start) // group

    def body(jj, carry):
        for r in range(group):
            tile(start + group * jj + r, r, False)
        return carry

    lax.fori_loop(0, n_groups, body, 0)

    def rest(j, carry):
        tile(j, 0, False)
        return carry

    lax.fori_loop(start + n_groups * group, i, rest, 0)

    lv = lam_ref[...]
    lam = (jnp.exp(jnp.sum(lv[0:1] * lv[1:2], axis=1, keepdims=True))
           - jnp.exp(jnp.sum(lv[2:3] * lv[3:4], axis=1, keepdims=True)) + lambda_init)
    l0 = jnp.sum(l_ref[0], axis=1, keepdims=True)
    l1 = jnp.sum(l_ref[1], axis=1, keepdims=True)
    o = (acc_ref[0] * (1.0 / jnp.maximum(l0, 1e-30))
         - lam * (acc_ref[1] * (1.0 / jnp.maximum(l1, 1e-30))))
    o = _rms(o, sg_ref[...]) * (1.0 - lambda_init)
    o_ref[0] = o.astype(BF16)


def _diff_attn(coefs, norms, qx, kx, v, lam_vecs, subln_g, lambda_init):
    B, S, _ = v.shape
    H = DIFF_HEADS
    tq = POS_PERIOD
    kern = functools.partial(_diff_attn_kernel, tq=tq, lambda_init=lambda_init)
    return pl.pallas_call(
        kern,
        grid_spec=pltpu.PrefetchScalarGridSpec(
            num_scalar_prefetch=2,
            grid=(B, H, S // tq),
            in_specs=[pl.BlockSpec((1, tq, DIFF_QK), lambda b, h, i, *_: (b, i, h)),
                      pl.BlockSpec((1, S, DIFF_QK), lambda b, h, i, *_: (b, 0, h)),
                      pl.BlockSpec((1, S, DIFF_V), lambda b, h, i, *_: (b, 0, h)),
                      pl.BlockSpec((4, HEAD_DIM), lambda b, h, i, *_: (0, 0)),
                      pl.BlockSpec((1, DIFF_V), lambda b, h, i, *_: (0, 0))],
            out_specs=pl.BlockSpec((1, tq, DIFF_V), lambda b, h, i, *_: (b, i, h)),
            scratch_shapes=([pltpu.VMEM((2, tq, tq), F32)] * DIFF_GROUP
                            + [pltpu.VMEM((2, tq, tq), BF16)] * DIFF_GROUP
                            + [pltpu.VMEM((2, tq, 128), F32)] * DIFF_GROUP
                            + [pltpu.VMEM((2, tq, 128), F32), pltpu.VMEM((2, tq, 128), F32),
                               pltpu.VMEM((2, tq, DIFF_V), F32)])),
        out_shape=jax.ShapeDtypeStruct((B, S, H * DIFF_V), BF16),
        compiler_params=_cparams(3),
        name="diff_attn",
    )(coefs, norms, qx, kx, v, lam_vecs, subln_g)


def _ffn_kernel(x_ref, o_ref, wo_ref, g_ref, wup_ref, cw_ref, cb_ref, wdn_ref, fg_ref, y_ref, ug_ref, uv_ref,
                *, tm, final_norm):
    i = pl.program_id(1)

    @pl.when(i == 0)
    def _():
        ug_ref[0:8, :] = jnp.zeros((8, D_FF), F32)
        uv_ref[0:8, :] = jnp.zeros((8, D_FF), F32)

    x = x_ref[0] + _dot(o_ref[0], wo_ref[...])
    h = _rms(x, g_ref[...]).astype(BF16)
    acc = jnp.zeros((tm, D_MODEL), F32)
    for c in range(D_FF // FF_CHUNK):
        halves = []
        for u_ref, col0 in ((ug_ref, FF_CHUNK * c), (uv_ref, D_FF + FF_CHUNK * c)):
            cols = slice(FF_CHUNK * c, FF_CHUNK * (c + 1))
            u_ref[8:8 + tm, cols] = _dot(h, wup_ref[:, col0:col0 + FF_CHUNK])
            w = cw_ref[:, col0:col0 + FF_CHUNK]
            y = (u_ref[6:6 + tm, cols] * w[0:1] + u_ref[7:7 + tm, cols] * w[1:2]
                 + u_ref[8:8 + tm, cols] * w[2:3] + cb_ref[:, col0:col0 + FF_CHUNK])
            u_ref[0:8, cols] = u_ref[tm:tm + 8, cols]
            halves.append(y)
        act = (jax.nn.silu(halves[0]) * halves[1]).astype(BF16)
        acc = acc + _dot(act, wdn_ref[FF_CHUNK * c:FF_CHUNK * (c + 1), :])
    out = x + acc
    if final_norm:
        out = _rms(out, fg_ref[...])
    y_ref[0] = out


def _ffn(x, o, w_o, g, w_up, conv_w, conv_b, w_down, final_g, final_norm, tm=512):
    B, S, D = x.shape
    kern = functools.partial(_ffn_kernel, tm=tm, final_norm=final_norm)
    const = lambda b, i: (0, 0)
    return pl.pallas_call(
        kern,
        grid=(B, S // tm),
        in_specs=[pl.BlockSpec((1, tm, D), lambda b, i: (b, i, 0)),
                  pl.BlockSpec((1, tm, o.shape[2]), lambda b, i: (b, i, 0)),
                  pl.BlockSpec(w_o.shape, const, pipeline_mode=pl.Buffered(1)),
                  pl.BlockSpec((1, D), const),
                  pl.BlockSpec(w_up.shape, const, pipeline_mode=pl.Buffered(1)),
                  pl.BlockSpec(conv_w.shape, const),
                  pl.BlockSpec(conv_b.shape, const),
                  pl.BlockSpec(w_down.shape, const, pipeline_mode=pl.Buffered(1)),
                  pl.BlockSpec((1, D), const)],
        out_specs=pl.BlockSpec((1, tm, D), lambda b, i: (b, i, 0)),
        out_shape=jax.ShapeDtypeStruct((B, S, D), F32),
        scratch_shapes=[pltpu.VMEM((tm + 8, D_FF), F32), pltpu.VMEM((tm + 8, D_FF), F32)],
        compiler_params=_cparams(2),
        name="conv_ffn",
    )(x, o, w_o, g, w_up, conv_w, conv_b, w_down, final_g)


def _alibi_slopes(n):
    return jnp.exp2(-8.0 * jnp.arange(1, n + 1, dtype=F32) / n)


def _nsa_weight(w_in):
    gates = w_in[:, NSA_QKV_COLS:].reshape(D_MODEL, NSA_GROUPS, NSA_HPG, 3)
    gates = gates.transpose(0, 1, 3, 2).reshape(D_MODEL, NSA_GROUPS, 3 * NSA_HPG)
    gates = jnp.pad(gates, ((0, 0), (0, 0), (0, NSA_GATE_STRIDE - 3 * NSA_HPG)))
    gates = gates.reshape(D_MODEL, NSA_GROUPS * NSA_GATE_STRIDE)
    gates = jnp.pad(gates, ((0, 0), (0, NSA_GATE_PAD - NSA_GROUPS * NSA_GATE_STRIDE)))

    def widen(cols):
        w = w_in[:, cols].reshape(D_MODEL, -1, HEAD_DIM)
        return jnp.pad(w, ((0, 0), (0, 0), (0, NSA_ROW - HEAD_DIM))).reshape(D_MODEL, -1)

    nq, nk = NSA_HEADS * HEAD_DIM, NSA_GROUPS * HEAD_DIM
    kc, vc, ks, vs, kw, vw = [slice(nq + nk * b, nq + nk * (b + 1)) for b in range(6)]
    return jnp.concatenate([widen(slice(0, nq)), widen(ks), widen(kw),
                            w_in[:, kc], w_in[:, vc], w_in[:, vs], w_in[:, vw], gates], axis=1).astype(BF16)


def _coef_pieces(coefs):
    c_hi = coefs.astype(BF16)
    c_mid = (coefs - c_hi.astype(F32)).astype(BF16)
    c_lo = (coefs - c_hi.astype(F32) - c_mid.astype(F32)).astype(BF16)
    return jnp.stack([c_hi, c_hi, c_mid, c_mid, c_lo, c_lo], axis=1)


def _nsa_columns(coefs, nc):
    def place(cols, lane0):
        return jnp.pad(cols.astype(F32), ((0, 0), (lane0, NSA_ROW - lane0 - cols.shape[1])))

    q_cols = place(_coef_pieces(coefs), COL_COEF)
    pos = jnp.arange(SEL_CHUNK)
    k_pos = jnp.stack([pos - pos % 2, pos % 2] * 3, axis=1)
    k_mask = jnp.where(pos[:, None] // SEL_BLOCK == jnp.arange(BLOCKS_PER_CHUNK)[None, :], MASK_BIG, 0.0)
    k_cols = place(k_pos, COL_COEF) + place(k_mask, COL_MASK)
    j = jnp.arange(nc)
    c_pos = jnp.stack([4 * CMP_STRIDE * (j // 4), CMP_STRIDE * (j % 4)] * 3, axis=1)
    cmp_cols = jnp.stack([place(c_pos, COL_COEF), jnp.zeros((nc, NSA_ROW), F32)])
    return q_cols, k_cols, cmp_cols


def _nsa_layer(x, norm_g, w_in, k_pe, k_w1, k_w2, v_pe, v_w1, v_w2, w_out):
    B, S, _ = x.shape
    nc = S // CMP_STRIDE
    coefs = _alibi_slopes(NSA_HEADS) * LOG2E
    q_cols, k_cols, cmp_cols = _nsa_columns(coefs, nc)
    q, k128, kv, gates, nsq = _nsa_proj(x, norm_g[None], _nsa_weight(w_in), q_cols, k_cols)
    n_q = (SEL_CHUNK // NSA_TQ) * NSA_HEADS
    norms = jnp.sqrt(jnp.concatenate([nsq[:, :, 0, :n_q].reshape(-1),
                                      nsq[:, :, 0, n_q:n_q + NSA_GROUPS].reshape(-1)]))

    xc = kv[0:2].reshape(2, B, NSA_GROUPS, nc, CMP_STRIDE * HEAD_DIM)
    pe8 = jnp.broadcast_to(jnp.stack([k_pe, v_pe]).reshape(2, 1, CMP_BLOCK * HEAD_DIM),
                           (2, 8, CMP_BLOCK * HEAD_DIM)).astype(BF16)
    w2 = jnp.pad(jnp.stack([k_w2, v_w2]), ((0, 0), (0, 0), (0, NSA_ROW - HEAD_DIM))).astype(BF16)
    cmp = _compress(xc, pe8, jnp.stack([k_w1, v_w1]).astype(BF16), w2, cmp_cols)
    kcmp, vcmp_t = cmp[0], cmp[1][..., :HEAD_DIM].transpose(0, 1, 3, 2)

    nsb = S // SEL_BLOCK
    ratio = SEL_BLOCK // CMP_STRIDE
    sb = jnp.arange(nsb)[:, None]
    cb = jnp.arange(nc)[None, :]
    pool_t = ((cb >= ratio * sb - 1) & (cb <= ratio * sb + ratio - 1)).astype(BF16)
    grp = (jnp.arange(nsb)[:, None] // BLOCKS_PER_CHUNK == jnp.arange(128)[None, :]).astype(BF16)
    ocmp, nsel, bits = _cmp_topk(q, kcmp, vcmp_t, pool_t, grp)
    sub = CMP_TQ // NSA_TQ
    word = (bits[:, 0, 0:2 * sub:2] | (bits[:, 0, 1:2 * sub:2] << 16)).reshape(-1)
    return _sel_win(word, coefs, norms, q, k128, kv, nsel, ocmp, gates), w_out.astype(BF16)


def _alibi_columns(coefs, period):
    pieces = _coef_pieces(coefs)
    pos = jnp.arange(period)
    parts = jnp.stack([pos - pos % 2, pos % 2] * 3, axis=1).astype(BF16)
    n = coefs.shape[0]
    pieces_b = jnp.broadcast_to(pieces[None], (period, n, 6))
    parts_b = jnp.broadcast_to(parts[:, None], (period, n, 6))
    pad = jnp.zeros((period, n, DIFF_V - 12), BF16)
    eq = jnp.concatenate([pieces_b, parts_b, pad], axis=2).reshape(period, n * DIFF_V)
    ek = jnp.concatenate([parts_b, -pieces_b, pad], axis=2).reshape(period, n * DIFF_V)
    return eq, ek


def _diff_layer(x, norm_g, w_in, lq1, lk1, lq2, lk2, subln_g, w_out, lambda_init):
    nqk = DIFF_HEADS * DIFF_V
    scale = jnp.concatenate([jnp.full((nqk,), HEAD_DIM ** -0.5 * LOG2E, F32), jnp.ones((2 * nqk,), F32)])
    coefs = _alibi_slopes(DIFF_HEADS) * LOG2E
    eq, ek = _alibi_columns(coefs, POS_PERIOD)
    qx, kx, v, nsq = _diff_proj(x, norm_g[None], (w_in * scale[None, :]).astype(BF16), eq, ek)
    nsq = nsq[:, :, 0, :2 * DIFF_HEADS].reshape(nsq.shape[0], nsq.shape[1], 2, DIFF_HEADS)
    norms = jnp.sqrt(nsq).transpose(2, 0, 3, 1).reshape(-1)
    lam_vecs = jnp.stack([lq1, lk1, lq2, lk2]).astype(F32)
    o = _diff_attn(coefs, norms, qx, kx, v, lam_vecs, subln_g[None].astype(F32), lambda_init)
    return o, w_out.astype(BF16)


def kernel(x, norm_mix_g, norm_ffn_g, final_norm_g, nsa_w_in, nsa_cmp_k_pe, nsa_cmp_k_w1, nsa_cmp_k_w2, nsa_cmp_v_pe, nsa_cmp_v_w1, nsa_cmp_v_w2, nsa_w_out, diff_w_in, diff_lam_q1, diff_lam_k1, diff_lam_q2, diff_lam_k2, diff_subln_g, diff_w_out, ffn_w_up, ffn_conv_w, ffn_conv_b, ffn_w_down):
    depth = norm_mix_g.shape[0]
    for i in range(depth):
        j = i // 2
        if i % 2 == 0:
            o, w_o = _nsa_layer(x, norm_mix_g[i], nsa_w_in[j], nsa_cmp_k_pe[j], nsa_cmp_k_w1[j], nsa_cmp_k_w2[j],
                                nsa_cmp_v_pe[j], nsa_cmp_v_w1[j], nsa_cmp_v_w2[j], nsa_w_out[j])
        else:
            lambda_init = 0.8 - 0.6 * math.exp(-0.3 * i)
            o, w_o = _diff_layer(x, norm_mix_g[i], diff_w_in[j], diff_lam_q1[j], diff_lam_k1[j], diff_lam_q2[j],
                                 diff_lam_k2[j], diff_subln_g[j], diff_w_out[j], lambda_init)
        x = _ffn(x, o, w_o, norm_ffn_g[i][None], ffn_w_up[i].astype(BF16), ffn_conv_w[i], ffn_conv_b[i][None],
                 ffn_w_down[i].astype(BF16), final_norm_g[None], final_norm=(i == depth - 1))
    return x
```
